```python
import math
import jax, jax.numpy as jnp
from jax import lax
import numpy as np

D_MODEL = 1024
BATCH = 8
SEQ = 4096
DEPTH = 1

N_META = 16
BLOCK = 128
PAD = BLOCK - N_META
CHUNK = 64
HG_HEADS = D_MODEL // 128
HG_DK = 128
HG_DV = 128
HG_KDIM = HG_HEADS * HG_DK
HG_WIDTH = HG_HEADS * HG_DV
DA_HEADS = D_MODEL // 128
DA_D = 64
DA_WIDTH = DA_HEADS * 2 * DA_D
ROPE_DIM = DA_D // 4
ROPE_THETA = 500000.0
EPS = 1e-6
PROJ_SIZES = (HG_KDIM, HG_KDIM, HG_WIDTH, HG_WIDTH,
              DA_WIDTH, DA_WIDTH, DA_WIDTH, DA_WIDTH, D_MODEL, D_MODEL)
PROJ_WIDTH = sum(PROJ_SIZES)

kernel_name = "hgrn2_diffattn_gated_hybrid"


def rmsnorm(x, g):
    xf = x.astype(jnp.float32)
    y = xf * lax.rsqrt(jnp.mean(xf * xf, axis=-1, keepdims=True) + EPS)
    return (y * g.astype(jnp.float32)).astype(x.dtype)


def split_heads(a, n_heads):
    b, l, w = a.shape
    return a.reshape(b, l, n_heads, w // n_heads).transpose(0, 2, 1, 3)


def merge_heads(a):
    b, h, l, dh = a.shape
    return a.transpose(0, 2, 1, 3).reshape(b, l, h * dh)


def rope_partial(x, pos):
    half = ROPE_DIM // 2
    inv = ROPE_THETA ** (-jnp.arange(half, dtype=jnp.float32) * 2.0 / ROPE_DIM)
    ang = pos.astype(jnp.float32)[:, None] * inv[None, :]
    cos = jnp.cos(ang).astype(x.dtype)
    sin = jnp.sin(ang).astype(x.dtype)
    x1, x2, xp = x[..., :half], x[..., half:ROPE_DIM], x[..., ROPE_DIM:]
    return jnp.concatenate([x1 * cos - x2 * sin, x2 * cos + x1 * sin, xp], axis=-1)


def hgrn2_chunked(q, logf, k, v):
    B, H, L, DK = q.shape
    DV = v.shape[-1]
    n = L // CHUNK

    def to_chunks(a):
        return a.reshape(B, H, n, CHUNK, a.shape[-1]).transpose(2, 0, 1, 3, 4)

    causal = jnp.tril(jnp.ones((CHUNK, CHUNK), dtype=bool))[None, None, :, :, None]

    def step(S, inp):
        qb, gb, kb, vb = inp
        b = jnp.cumsum(gb, axis=2)
        rel = b[:, :, :, None, :] - b[:, :, None, :, :]
        dec = jnp.where(causal, jnp.exp(jnp.minimum(rel, 0.0)), 0.0)
        A = jnp.einsum('bhtd,bhsd,bhtsd->bhts', qb, kb, dec)
        o = (jnp.einsum('bhts,bhsv->bhtv', A, vb)
             + jnp.einsum('bhtd,bhdv->bhtv', qb * jnp.exp(b), S))
        bl = b[:, :, -1:, :]
        S_new = (jnp.exp(bl[:, :, 0, :])[..., None] * S
                 + jnp.einsum('bhsd,bhsv->bhdv', kb * jnp.exp(bl - b), vb))
        return S_new, o

    S0 = jnp.zeros((B, H, DK, DV), jnp.float32)
    _, oc = lax.scan(step, S0, (to_chunks(q), to_chunks(logf), to_chunks(k), to_chunks(v)))
    return oc.transpose(1, 2, 0, 3, 4).reshape(B, H, L, DV)


def diff_attention(q1, k1, q2, k2, v, lam):
    L = q1.shape[2]
    d = q1.shape[-1]
    n = L // BLOCK
    scale = d ** -0.5
    key_idx = jnp.arange(L)
    valid_key = key_idx >= PAD

    def block(i):
        s0 = i * BLOCK
        qb1 = lax.dynamic_slice_in_dim(q1, s0, BLOCK, axis=2)
        qb2 = lax.dynamic_slice_in_dim(q2, s0, BLOCK, axis=2)
        qidx = s0 + jnp.arange(BLOCK)
        mask = valid_key[None, :] & (key_idx[None, :] <= qidx[:, None])

        def probs(qb, kk):
            s = jnp.einsum('bhqd,bhkd->bhqk', qb, kk).astype(jnp.float32) * scale
            return jax.nn.softmax(jnp.where(mask, s, -1e30), axis=-1)

        p = probs(qb1, k1) - lam * probs(qb2, k2)
        return jnp.einsum('bhqk,bhkv->bhqv', p.astype(v.dtype), v)

    out = lax.map(block, jnp.arange(n))
    B, H = q1.shape[0], q1.shape[1]
    return out.transpose(1, 2, 0, 3, 4).reshape(B, H, L, v.shape[-1])


def setup_inputs(seed: int = 0) -> dict:
    key = jax.random.key(seed)
    ks = jax.random.split(key, 14)
    f32 = jnp.float32
    nrm = lambda k, shape, s: jax.random.normal(k, shape, f32) * s
    return {
        "x": nrm(ks[0], (BATCH, SEQ, D_MODEL), 1.0),
        "meta_tokens": nrm(ks[1], (N_META, D_MODEL), 1.0),
        "norm_g": 1.0 + nrm(ks[2], (DEPTH, D_MODEL), 0.02),
        "w_in": nrm(ks[3], (DEPTH, D_MODEL, PROJ_WIDTH), D_MODEL ** -0.5),
        "hg_lb_logits": nrm(ks[4], (DEPTH + 1, HG_KDIM), 0.5),
        "hg_norm_g": 1.0 + nrm(ks[5], (DEPTH, HG_DV), 0.02),
        "da_lambda": nrm(ks[6], (DEPTH, 4, DA_D), 0.1),
        "da_norm_g": 1.0 + nrm(ks[7], (DEPTH, 2 * DA_D), 0.02),
        "w_branch_a": nrm(ks[8], (DEPTH, HG_WIDTH, D_MODEL), HG_WIDTH ** -0.5),
        "w_branch_b": nrm(ks[9], (DEPTH, DA_WIDTH, D_MODEL), DA_WIDTH ** -0.5),
        "w_out": nrm(ks[10], (DEPTH, D_MODEL, D_MODEL), D_MODEL ** -0.5),
        "final_g": 1.0 + nrm(ks[11], (D_MODEL,), 0.02),
    }


def reference(x, meta_tokens, norm_g, w_in, hg_lb_logits, hg_norm_g, da_lambda,
              da_norm_g, w_branch_a, w_branch_b, w_out, final_g):
    B, S, D = x.shape
    dt = x.dtype
    h = jnp.concatenate([jnp.zeros((B, PAD, D), dt),
                         jnp.broadcast_to(meta_tokens.astype(dt)[None], (B, N_META, D)),
                         x], axis=1)
    Lp = h.shape[1]
    idx = jnp.arange(Lp)
    valid = idx >= PAD
    pos = jnp.maximum(idx - PAD, 0)
    offs = [int(o) for o in np.cumsum(PROJ_SIZES)[:-1]]
    lb_all = jnp.cumsum(jax.nn.softmax(hg_lb_logits.astype(jnp.float32), axis=0), axis=0)

    for l in range(DEPTH):
        u = rmsnorm(h, norm_g[l])
        proj = jnp.einsum('bld,de->ble', u, w_in[l])
        hq, hf, hi, hz, aq, ak, av, az, ga, gb = jnp.split(proj, offs, axis=-1)

        lb = lb_all[l]
        f = lb + (1.0 - lb) * jax.nn.sigmoid(hf.astype(jnp.float32))
        f = jnp.where(valid[None, :, None], f, 1.0)
        kA = 1.0 - f
        oA = hgrn2_chunked(split_heads(hq.astype(jnp.float32), HG_HEADS),
                           split_heads(jnp.log(f), HG_HEADS),
                           split_heads(kA, HG_HEADS),
                           split_heads(hi.astype(jnp.float32), HG_HEADS))
        oA = merge_heads(rmsnorm(oA.astype(dt), hg_norm_g[l]))
        yA = jnp.einsum('blw,wd->bld', oA * jax.nn.silu(hz), w_branch_a[l])

        lam_init = 0.8 - 0.6 * math.exp(-0.3 * l)
        lp = da_lambda[l].astype(jnp.float32)
        lam = jnp.exp(jnp.sum(lp[0] * lp[1])) - jnp.exp(jnp.sum(lp[2] * lp[3])) + lam_init
        qh = split_heads(aq, DA_HEADS)
        kh = split_heads(ak, DA_HEADS)
        vh = split_heads(av, DA_HEADS)
        q1 = rope_partial(qh[..., :DA_D], pos)
        q2 = rope_partial(qh[..., DA_D:], pos)
        k1 = rope_partial(kh[..., :DA_D], pos)
        k2 = rope_partial(kh[..., DA_D:], pos)
        oB = diff_attention(q1, k1, q2, k2, vh, lam)
        oB = merge_heads(rmsnorm(oB, da_norm_g[l]) * (1.0 - lam_init))
        yB = jnp.einsum('blw,wd->bld', oB * jax.nn.silu(az), w_branch_b[l])

        m = jax.nn.sigmoid(ga) * yA + jax.nn.sigmoid(gb) * yB
        h = h + jnp.einsum('bld,de->ble', m, w_out[l])

    out = rmsnorm(h, final_g)
    return out[:, PAD + N_META:, :]
```

```python
import functools

import jax
import jax.numpy as jnp
import numpy as np
from jax import lax
from jax.experimental import pallas as pl
from jax.experimental.pallas import tpu as pltpu

F32 = jnp.float32
BF16 = jnp.bfloat16

D_MODEL = 1024
N_HEADS = 8
HEAD_W = 128
DA_D = 64
N_META = 16
PREFIX = 128
N_INERT = PREFIX - N_META
ROPE_DIM = 16
ROPE_HALF = ROPE_DIM // 2
ROPE_THETA = 500000.0
EPS = 1e-6
LAM_INIT = 0.2
MASK_VALUE = -1e30

SEG_W = 1024
N_SEG = 10
SEG_HQ, SEG_HF, SEG_HI, SEG_HZ, SEG_AQ, SEG_AK, SEG_AV, SEG_AZ, SEG_GA, SEG_GB = range(N_SEG)

CHUNK = 128
SUB = 16
N_SUB = CHUNK // SUB
ATT_BQ = 256

VMEM_LIMIT = 56 * 1024 * 1024


def _sigmoid(x):
    return 1.0 / (1.0 + jnp.exp(-x))


def _inproj_kernel(n_inert, x_ref, ng_ref, w_ref, lb_ref, cos_ref, sa_ref, sb_ref,
                   out_ref, glog_ref, u_ref):
    j = pl.program_id(1)

    @pl.when(j == 0)
    def _():
        x = x_ref[...]
        ms = jnp.mean(x * x, axis=-1, keepdims=True)
        u_ref[...] = (x * lax.rsqrt(ms + EPS) * ng_ref[...]).astype(BF16)

    acc = jnp.dot(u_ref[...], w_ref[...], preferred_element_type=F32)
    is_rope = jnp.logical_or(j == SEG_AQ, j == SEG_AK)
    is_gate = j == SEG_HF

    @pl.when(is_gate)
    def _():
        lb = lb_ref[...]
        f = lb + (1.0 - lb) * _sigmoid(acc)
        if n_inert:
            rows = lax.broadcasted_iota(jnp.int32, acc.shape, 0)
            f = jnp.where(rows >= n_inert, f, 1.0)
        glog_ref[...] = jnp.log(f)
        out_ref[...] = (1.0 - f).astype(BF16)

    @pl.when(is_rope)
    def _():
        scale = jnp.where(j == SEG_AQ, DA_D ** -0.5, 1.0).astype(F32)
        cos = cos_ref[...]
        sa = sa_ref[...]
        sb = sb_ref[...]
        for c in range(SEG_W // HEAD_W):
            xs = acc[:, c * HEAD_W:(c + 1) * HEAD_W]
            y = (xs * cos + pltpu.roll(xs, ROPE_HALF, 1) * sa
                 + pltpu.roll(xs, HEAD_W - ROPE_HALF, 1) * sb)
            out_ref[:, c * HEAD_W:(c + 1) * HEAD_W] = (y * scale).astype(BF16)

    @pl.when(jnp.logical_not(jnp.logical_or(is_rope, is_gate)))
    def _():
        out_ref[...] = acc.astype(BF16)


def _inproj(x2d, norm_g, w_bf, lb, cos_t, sa_t, sb_t, *, tm, rows_per_seq, n_inert):
    n_rows = x2d.shape[0]
    n_tiles = n_rows // tm
    tiles_per_seq = rows_per_seq // tm
    row_spec = pl.BlockSpec((1, D_MODEL), lambda i, j: (0, 0))
    tab_spec = pl.BlockSpec((tm, HEAD_W), lambda i, j: (i % tiles_per_seq, 0))
    return pl.pallas_call(
        functools.partial(_inproj_kernel, n_inert),
        grid=(n_tiles, N_SEG),
        in_specs=[
            pl.BlockSpec((tm, D_MODEL), lambda i, j: (i, 0)),
            row_spec,
            pl.BlockSpec((D_MODEL, SEG_W), lambda i, j: (0, j)),
            row_spec,
            tab_spec, tab_spec, tab_spec,
        ],
        out_specs=[
            pl.BlockSpec((tm, SEG_W), lambda i, j: (i, j)),
            pl.BlockSpec((tm, SEG_W), lambda i, j: (i, 0)),
        ],
        out_shape=[
            jax.ShapeDtypeStruct((n_rows, N_SEG * SEG_W), BF16),
            jax.ShapeDtypeStruct((n_rows, SEG_W), F32),
        ],
        scratch_shapes=[pltpu.VMEM((tm, D_MODEL), BF16)],
        compiler_params=pltpu.CompilerParams(
            dimension_semantics=("parallel", "arbitrary"), vmem_limit_bytes=VMEM_LIMIT),
        name="inproj",
    )(x2d, norm_g, w_bf, lb, cos_t, sa_t, sb_t)


def _bcast_rows(rows, n):
    return jnp.concatenate([jnp.broadcast_to(r, (n, r.shape[1])) for r in rows], axis=0)


def _hgrn_chunk(q, k, v, g, st, tril):
    g_hi = g.astype(BF16)
    r1 = g - g_hi.astype(F32)
    g_mid = r1.astype(BF16)
    g_lo = (r1 - g_mid.astype(F32)).astype(BF16)
    bs = jnp.dot(tril.astype(BF16), jnp.concatenate([g_hi, g_mid, g_lo], axis=1),
                 preferred_element_type=F32)
    dk = q.shape[1]
    b = bs[:, :dk] + bs[:, dk:2 * dk] + bs[:, 2 * dk:]

    ends = [b[SUB * s + SUB - 1:SUB * s + SUB, :] for s in range(N_SUB)]
    starts = [jnp.zeros_like(ends[0])] + ends[:-1]
    end_b = _bcast_rows(ends, SUB)
    start_b = _bcast_rows(starts, SUB)
    b_last = ends[-1]

    qd = q * jnp.exp(b - start_b)
    khat = k * jnp.exp(end_b - b)
    q_s = (qd * _bcast_rows([jnp.exp(s) for s in starts], SUB)).astype(BF16)
    k_s = (khat * _bcast_rows([jnp.exp(b_last - e) for e in ends], SUB)).astype(BF16)

    qd_bf = qd.astype(BF16)
    rows = []
    for i in range(N_SUB):
        parts = []
        for jj in range(i + 1):
            kj = khat[SUB * jj:SUB * jj + SUB, :]
            if jj == i:
                w = jnp.exp(starts[i] - ends[i])
            elif jj == i - 1:
                w = None
            else:
                w = jnp.exp(starts[i] - ends[jj])
            parts.append(kj if w is None else kj * w)
        if i + 1 < N_SUB:
            parts.append(jnp.zeros((CHUNK - SUB * (i + 1), dk), F32))
        km = jnp.concatenate(parts, axis=0).astype(BF16)
        rows.append(lax.dot_general(qd_bf[SUB * i:SUB * i + SUB, :], km,
                                    (((1,), (1,)), ((), ())), preferred_element_type=F32))
    a = jnp.where(tril, jnp.concatenate(rows, axis=0), 0.0).astype(BF16)

    o = (jnp.dot(a, v, preferred_element_type=F32)
         + lax.dot_general(q_s, st.astype(BF16), (((1,), (1,)), ((), ())),
                           preferred_element_type=F32))
    st_new = st * jnp.exp(b_last) + lax.dot_general(
        v, k_s, (((0,), (0,)), ((), ())), preferred_element_type=F32)
    return o, st_new


def _hgrn_kernel(pq_ref, pk_ref, pv_ref, pg_ref, q_ref, k_ref, v_ref, z_ref, g_ref, gain_ref, o_ref):
    n_chunks = q_ref.shape[0] // CHUNK
    tril = (lax.broadcasted_iota(jnp.int32, (CHUNK, CHUNK), 0)
            >= lax.broadcasted_iota(jnp.int32, (CHUNK, CHUNK), 1))
    gain = gain_ref[...]

    st0 = jnp.zeros((HEAD_W, HEAD_W), F32)
    _, st0 = _hgrn_chunk(pq_ref[...].astype(F32), pk_ref[...].astype(F32), pv_ref[...],
                         pg_ref[...], st0, tril)

    def body(c, st):
        r0 = pl.multiple_of(c * CHUNK, CHUNK)
        sl = pl.ds(r0, CHUNK)
        o, st = _hgrn_chunk(q_ref[sl, :].astype(F32), k_ref[sl, :].astype(F32), v_ref[sl, :],
                            g_ref[sl, :], st, tril)
        ms = jnp.mean(o * o, axis=-1, keepdims=True)
        on = o * lax.rsqrt(ms + EPS) * gain
        z = z_ref[sl, :].astype(F32)
        o_ref[sl, :] = (on * (z * _sigmoid(z))).astype(BF16)
        return st

    lax.fori_loop(0, n_chunks, body, st0)


def _hgrn(proj, glog, pproj, pglog, gain, *, batch, seq):
    def seg_spec(seg):
        return pl.BlockSpec((seq, HEAD_W), lambda b, h: (b, seg * N_HEADS + h))

    def pseg_spec(seg):
        return pl.BlockSpec((PREFIX, HEAD_W), lambda b, h: (0, seg * N_HEADS + h))

    return pl.pallas_call(
        _hgrn_kernel,
        grid=(batch, N_HEADS),
        in_specs=[
            pseg_spec(SEG_HQ), pseg_spec(SEG_HF), pseg_spec(SEG_HI),
            pl.BlockSpec((PREFIX, HEAD_W), lambda b, h: (0, h)),
            seg_spec(SEG_HQ), seg_spec(SEG_HF), seg_spec(SEG_HI), seg_spec(SEG_HZ),
            pl.BlockSpec((seq, HEAD_W), lambda b, h: (b, h)),
            pl.BlockSpec((1, HEAD_W), lambda b, h: (0, 0)),
        ],
        out_specs=pl.BlockSpec((seq, HEAD_W), lambda b, h: (b, h)),
        out_shape=jax.ShapeDtypeStruct((batch * seq, SEG_W), BF16),
        compiler_params=pltpu.CompilerParams(
            dimension_semantics=("parallel", "parallel"), vmem_limit_bytes=VMEM_LIMIT),
        name="hgrn2",
    )(pproj, pproj, pproj, pglog, proj, proj, proj, proj, glog, gain)


def _attn_kernel(lam_ref, pk_ref, pv_ref, q_ref, k_ref, v_ref, z_ref, gain_ref, o_ref):
    bq = ATT_BQ
    n_blocks = q_ref.shape[0] // bq
    lam = lam_ref[0, 0]
    gain = gain_ref[...]
    lane = lax.broadcasted_iota(jnp.int32, (1, HEAD_W), 1)
    first = lane < DA_D
    row = lax.broadcasted_iota(jnp.int32, (2 * bq, bq), 0)
    col = lax.broadcasted_iota(jnp.int32, (2 * bq, bq), 1)
    causal = col <= jnp.where(row >= bq, row - bq, row)
    pcol = lax.broadcasted_iota(jnp.int32, (2 * bq, PREFIX), 1)
    pvalid = pcol >= N_INERT
    nt = (((1,), (1,)), ((), ()))

    def qblock(i, carry):
        r0 = pl.multiple_of(i * bq, bq)
        q = q_ref[pl.ds(r0, bq), :]
        qq = jnp.concatenate([jnp.where(first, q, jnp.zeros_like(q)),
                              jnp.where(first, jnp.zeros_like(q), q)], axis=0)

        s = lax.dot_general(qq, pk_ref[...], nt, preferred_element_type=F32)
        s = jnp.where(pvalid, s, MASK_VALUE)
        m = jnp.max(s, axis=-1, keepdims=True)
        p = jnp.exp(s - m)
        l = jnp.sum(p, axis=-1, keepdims=True)
        acc = jnp.dot(p.astype(BF16), pv_ref[...], preferred_element_type=F32)

        def update(j, m, l, acc, masked):
            c0 = pl.multiple_of(j * bq, bq)
            kb = k_ref[pl.ds(c0, bq), :]
            vb = v_ref[pl.ds(c0, bq), :]
            s = lax.dot_general(qq, kb, nt, preferred_element_type=F32)
            if masked:
                s = jnp.where(causal, s, MASK_VALUE)
            m_new = jnp.maximum(m, jnp.max(s, axis=-1, keepdims=True))
            alpha = jnp.exp(m - m_new)
            p = jnp.exp(s - m_new)
            l = alpha * l + jnp.sum(p, axis=-1, keepdims=True)
            acc = alpha * acc + jnp.dot(p.astype(BF16), vb, preferred_element_type=F32)
            return m_new, l, acc

        m, l, acc = lax.fori_loop(0, i, lambda j, c: update(j, *c, masked=False), (m, l, acc))
        m, l, acc = update(i, m, l, acc, masked=True)

        out = acc / l
        o = out[:bq, :] - lam * out[bq:, :]
        ms = jnp.mean(o * o, axis=-1, keepdims=True)
        on = o * lax.rsqrt(ms + EPS) * gain * (1.0 - LAM_INIT)
        z = z_ref[pl.ds(r0, bq), :].astype(F32)
        o_ref[pl.ds(r0, bq), :] = (on * (z * _sigmoid(z))).astype(BF16)
        return carry

    lax.fori_loop(0, n_blocks, qblock, 0)


def _attn(proj, pproj, lam, gain, *, batch, seq):
    def seg_spec(seg):
        return pl.BlockSpec((seq, HEAD_W), lambda b, h: (b, seg * N_HEADS + h))

    def pseg_spec(seg):
        return pl.BlockSpec((PREFIX, HEAD_W), lambda b, h: (0, seg * N_HEADS + h))

    return pl.pallas_call(
        _attn_kernel,
        grid=(batch, N_HEADS),
        in_specs=[
            pl.BlockSpec(memory_space=pltpu.SMEM),
            pseg_spec(SEG_AK), pseg_spec(SEG_AV),
            seg_spec(SEG_AQ), seg_spec(SEG_AK), seg_spec(SEG_AV), seg_spec(SEG_AZ),
            pl.BlockSpec((1, HEAD_W), lambda b, h: (0, 0)),
        ],
        out_specs=pl.BlockSpec((seq, HEAD_W), lambda b, h: (b, h)),
        out_shape=jax.ShapeDtypeStruct((batch * seq, SEG_W), BF16),
        compiler_params=pltpu.CompilerParams(
            dimension_semantics=("parallel", "parallel"), vmem_limit_bytes=VMEM_LIMIT),
        name="diff_attn",
    )(lam, pproj, pproj, proj, proj, proj, proj, gain)


def _out_kernel(x_ref, oa_ref, ob_ref, ga_ref, gb_ref, wa_ref, wb_ref, wo_ref, fg_ref, out_ref):
    ya = jnp.dot(oa_ref[...], wa_ref[...], preferred_element_type=F32)
    yb = jnp.dot(ob_ref[...], wb_ref[...], preferred_element_type=F32)
    m = _sigmoid(ga_ref[...].astype(F32)) * ya + _sigmoid(gb_ref[...].astype(F32)) * yb
    h = x_ref[...] + jnp.dot(m.astype(BF16), wo_ref[...], preferred_element_type=F32)
    ms = jnp.mean(h * h, axis=-1, keepdims=True)
    out_ref[...] = h * lax.rsqrt(ms + EPS) * fg_ref[...]


def _out_stage(x2d, oa, ob, proj, wa, wb, wo, fg, *, tm):
    n_rows = x2d.shape[0]
    tile = lambda col: pl.BlockSpec((tm, SEG_W), lambda i: (i, col))
    wspec = pl.BlockSpec((SEG_W, D_MODEL), lambda i: (0, 0))
    return pl.pallas_call(
        _out_kernel,
        grid=(n_rows // tm,),
        in_specs=[tile(0), tile(0), tile(0), tile(SEG_GA), tile(SEG_GB), wspec, wspec, wspec,
                  pl.BlockSpec((1, D_MODEL), lambda i: (0, 0))],
        out_specs=tile(0),
        out_shape=jax.ShapeDtypeStruct((n_rows, D_MODEL), F32),
        compiler_params=pltpu.CompilerParams(
            dimension_semantics=("parallel",), vmem_limit_bytes=VMEM_LIMIT),
        name="out_stage",
    )(x2d, oa, ob, proj, proj, wa, wb, wo, fg)


def _rope_tables(pos):
    inv = ROPE_THETA ** (-jnp.arange(ROPE_HALF, dtype=F32) * 2.0 / ROPE_DIM)
    ang = pos.astype(F32)[:, None] * inv[None, :]
    cos, sin = jnp.cos(ang), jnp.sin(ang)
    n = pos.shape[0]
    ones = jnp.ones((n, DA_D - ROPE_DIM), F32)
    zeros8 = jnp.zeros((n, ROPE_HALF), F32)
    zeros = jnp.zeros((n, DA_D - ROPE_DIM), F32)
    cos_c = jnp.concatenate([cos, cos, ones], axis=1)
    sa_c = jnp.concatenate([zeros8, sin, zeros], axis=1)
    sb_c = jnp.concatenate([-sin, zeros8, zeros], axis=1)
    rep = lambda t: jnp.concatenate([t, t], axis=1)
    return rep(cos_c), rep(sa_c), rep(sb_c)


def kernel(x, meta_tokens, norm_g, w_in, hg_lb_logits, hg_norm_g, da_lambda, da_norm_g,
           w_branch_a, w_branch_b, w_out, final_g):
    batch, seq, d = x.shape
    assert d == D_MODEL and seq % ATT_BQ == 0 and seq % 1024 == 0
    assert norm_g.shape[0] == 1, "single-layer block"

    x2d = x.reshape(batch * seq, d)
    prefix = jnp.concatenate([jnp.zeros((N_INERT, d), x.dtype), meta_tokens.astype(x.dtype)], axis=0)

    lb = jnp.cumsum(jax.nn.softmax(hg_lb_logits.astype(F32), axis=0), axis=0)[0][None, :]
    lp = da_lambda[0].astype(F32)
    lam = (jnp.exp(jnp.sum(lp[0] * lp[1])) - jnp.exp(jnp.sum(lp[2] * lp[3])) + LAM_INIT).reshape(1, 1)

    w_bf = w_in[0].astype(BF16)
    ng = norm_g[0][None, :].astype(F32)
    tabs_main = _rope_tables(jnp.arange(seq) + N_META)
    tabs_pre = _rope_tables(jnp.maximum(jnp.arange(PREFIX) - N_INERT, 0))

    proj, glog = _inproj(x2d, ng, w_bf, lb, *tabs_main, tm=1024, rows_per_seq=seq, n_inert=0)
    pproj, pglog = _inproj(prefix, ng, w_bf, lb, *tabs_pre, tm=PREFIX, rows_per_seq=PREFIX,
                           n_inert=N_INERT)

    oa = _hgrn(proj, glog, pproj, pglog, hg_norm_g[0][None, :].astype(F32), batch=batch, seq=seq)
    ob = _attn(proj, pproj, lam, da_norm_g[0][None, :].astype(F32), batch=batch, seq=seq)

    out = _out_stage(x2d, oa, ob, proj, w_branch_a[0].astype(BF16), w_branch_b[0].astype(BF16),
                     w_out[0].astype(BF16), final_g[None, :].astype(F32), tm=512)
    return out.reshape(batch, seq, d)
```

```python
import functools

import jax
import jax.numpy as jnp
import numpy as np
from jax import lax
from jax.experimental import pallas as pl
from jax.experimental.pallas import tpu as pltpu

F32 = jnp.float32
BF16 = jnp.bfloat16

D_MODEL = 1024
N_HEADS = 8
HEAD_W = 128
DA_D = 64
N_META = 16
PREFIX = 128
N_INERT = PREFIX - N_META
ROPE_DIM = 16
ROPE_HALF = ROPE_DIM // 2
ROPE_THETA = 500000.0
EPS = 1e-6
LAM_INIT = 0.2
MASK_VALUE = -1e30
Q_SCALE = DA_D ** -0.5 * float(np.log2(np.e))

SEG_W = 1024
N_SEG = 9
SEG_HQ, SEG_HF, SEG_HI, SEG_HZ, SEG_AQ, SEG_AK, SEG_AZ, SEG_GA, SEG_GB = range(N_SEG)
REF_SEG_ORDER = (0, 1, 2, 3, 4, 5, 7, 8, 9)
REF_SEG_AV = 6

CHUNK = 128
SUB = 16
N_SUB = CHUNK // SUB
ATT_BQ = 512
ATT_BK = 512

VMEM_LIMIT = 56 * 1024 * 1024


def _sigmoid(x):
    return 1.0 / (1.0 + jnp.exp(-x))


def _inproj_kernel(n_inert, x_ref, ng_ref, w_ref, lb_ref, cos_ref, sa_ref, sb_ref,
                   out_ref, glog_ref, vt_ref, u_ref):
    j = pl.program_id(1)

    @pl.when(j == 0)
    def _():
        x = x_ref[...]
        ms = jnp.mean(x * x, axis=-1, keepdims=True)
        u_ref[...] = (x * lax.rsqrt(ms + EPS) * ng_ref[...]).astype(BF16)

    is_rope = jnp.logical_or(j == SEG_AQ, j == SEG_AK)
    is_gate = j == SEG_HF
    is_vt = j == N_SEG

    def project():
        return jnp.dot(u_ref[...], w_ref[...], preferred_element_type=F32)

    @pl.when(is_gate)
    def _():
        acc = project()
        lb = lb_ref[...]
        f = lb + (1.0 - lb) * _sigmoid(acc)
        if n_inert:
            rows = lax.broadcasted_iota(jnp.int32, acc.shape, 0)
            f = jnp.where(rows >= n_inert, f, 1.0)
        glog_ref[...] = jnp.log(f)
        out_ref[...] = (1.0 - f).astype(BF16)

    @pl.when(is_rope)
    def _():
        acc = project()
        scale = jnp.where(j == SEG_AQ, Q_SCALE, 1.0).astype(F32)
        cos = cos_ref[...]
        sa = sa_ref[...]
        sb = sb_ref[...]
        for c in range(SEG_W // HEAD_W):
            xs = acc[:, c * HEAD_W:(c + 1) * HEAD_W]
            y = (xs * cos + pltpu.roll(xs, ROPE_HALF, 1) * sa
                 + pltpu.roll(xs, HEAD_W - ROPE_HALF, 1) * sb)
            out_ref[:, c * HEAD_W:(c + 1) * HEAD_W] = (y * scale).astype(BF16)

    @pl.when(is_vt)
    def _():
        acc_t = lax.dot_general(w_ref[...], u_ref[...], (((1,), (1,)), ((), ())),
                                preferred_element_type=F32)
        kb = vt_ref.shape[2]
        for c in range(vt_ref.shape[0]):
            vt_ref[c] = acc_t[:, c * kb:(c + 1) * kb].astype(BF16)

    @pl.when(jnp.logical_not(jnp.logical_or(jnp.logical_or(is_rope, is_gate), is_vt)))
    def _():
        out_ref[...] = project().astype(BF16)


def _inproj(x2d, norm_g, w_ext, lb, cos_t, sa_t, sb_t, *, tm, rows_per_seq, n_inert):
    n_rows = x2d.shape[0]
    n_tiles = n_rows // tm
    tiles_per_seq = rows_per_seq // tm
    kb = min(tm, ATT_BK)
    row_spec = pl.BlockSpec((1, D_MODEL), lambda i, j: (0, 0))
    tab_spec = pl.BlockSpec((tm, HEAD_W), lambda i, j: (i % tiles_per_seq, 0))
    return pl.pallas_call(
        functools.partial(_inproj_kernel, n_inert),
        grid=(n_tiles, N_SEG + 1),
        in_specs=[
            pl.BlockSpec((tm, D_MODEL), lambda i, j: (i, 0)),
            row_spec,
            pl.BlockSpec((D_MODEL, SEG_W), lambda i, j: (0, j)),
            row_spec,
            tab_spec, tab_spec, tab_spec,
        ],
        out_specs=[
            pl.BlockSpec((tm, SEG_W), lambda i, j: (i, jnp.minimum(j, N_SEG - 1))),
            pl.BlockSpec((tm, SEG_W), lambda i, j: (i, 0)),
            pl.BlockSpec((tm // kb, D_MODEL, kb), lambda i, j: (i, 0, 0)),
        ],
        out_shape=[
            jax.ShapeDtypeStruct((n_rows, N_SEG * SEG_W), BF16),
            jax.ShapeDtypeStruct((n_rows, SEG_W), F32),
            jax.ShapeDtypeStruct((n_rows // kb, D_MODEL, kb), BF16),
        ],
        scratch_shapes=[pltpu.VMEM((tm, D_MODEL), BF16)],
        compiler_params=pltpu.CompilerParams(
            dimension_semantics=("parallel", "arbitrary"), vmem_limit_bytes=VMEM_LIMIT),
        name="inproj",
    )(x2d, norm_g, w_ext, lb, cos_t, sa_t, sb_t)


def _bcast_rows(rows, n):
    return jnp.concatenate([jnp.broadcast_to(r, (n, r.shape[1])) for r in rows], axis=0)


def _hgrn_chunk(q, k, v, g, sts, tril):
    n_heads = len(sts)
    width = q.shape[1]
    head = lambda x, h: x[:, h * HEAD_W:(h + 1) * HEAD_W]
    nt = (((1,), (1,)), ((), ()))

    g_hi = g.astype(BF16)
    r1 = g - g_hi.astype(F32)
    g_mid = r1.astype(BF16)
    g_lo = (r1 - g_mid.astype(F32)).astype(BF16)
    bs = jnp.dot(tril.astype(BF16), jnp.concatenate([g_hi, g_mid, g_lo], axis=1),
                 preferred_element_type=F32)
    b = bs[:, :width] + bs[:, width:2 * width] + bs[:, 2 * width:]

    ends = [b[SUB * s + SUB - 1:SUB * s + SUB, :] for s in range(N_SUB)]
    starts = [jnp.zeros_like(ends[0])] + ends[:-1]
    end_b = _bcast_rows(ends, SUB)
    start_b = _bcast_rows(starts, SUB)
    b_last = ends[-1]

    qd = q * jnp.exp(b - start_b)
    khat = k * jnp.exp(end_b - b)
    q_s = (qd * _bcast_rows([jnp.exp(s) for s in starts], SUB)).astype(BF16)
    k_s = (khat * _bcast_rows([jnp.exp(b_last - e) for e in ends], SUB)).astype(BF16)
    decay = jnp.exp(b_last)

    o_inter = [lax.dot_general(head(q_s, h), sts[h].astype(BF16), nt, preferred_element_type=F32)
               for h in range(n_heads)]
    st_new = [sts[h] * head(decay, h)
              + lax.dot_general(head(v, h), head(k_s, h), (((0,), (0,)), ((), ())),
                                preferred_element_type=F32)
              for h in range(n_heads)]

    qd_bf = qd.astype(BF16)
    a_rows = [[] for _ in range(n_heads)]
    for i in range(N_SUB):
        parts = []
        for jj in range(i + 1):
            kj = khat[SUB * jj:SUB * jj + SUB, :]
            if jj == i:
                w = jnp.exp(starts[i] - ends[i])
            elif jj == i - 1:
                w = None
            else:
                w = jnp.exp(starts[i] - ends[jj])
            parts.append(kj if w is None else kj * w)
        if i + 1 < N_SUB:
            parts.append(jnp.zeros((CHUNK - SUB * (i + 1), width), F32))
        km = jnp.concatenate(parts, axis=0).astype(BF16)
        qi = qd_bf[SUB * i:SUB * i + SUB, :]
        for h in range(n_heads):
            a_rows[h].append(lax.dot_general(head(qi, h), head(km, h), nt,
                                             preferred_element_type=F32))
    outs = []
    for h in range(n_heads):
        a = jnp.where(tril, jnp.concatenate(a_rows[h], axis=0), 0.0).astype(BF16)
        outs.append(jnp.dot(a, head(v, h), preferred_element_type=F32) + o_inter[h])
    return outs, st_new


def _hgrn_kernel(pq_ref, pk_ref, pv_ref, pg_ref, q_ref, k_ref, v_ref, z_ref, g_ref, gain_ref, o_ref,
                 st_ref):
    n_chunks = q_ref.shape[0] // CHUNK
    tril = (lax.broadcasted_iota(jnp.int32, (CHUNK, CHUNK), 0)
            >= lax.broadcasted_iota(jnp.int32, (CHUNK, CHUNK), 1))
    gain = gain_ref[...]

    @pl.when(pl.program_id(1) == 0)
    def _():
        _, sts = _hgrn_chunk(pq_ref[...].astype(F32), pk_ref[...].astype(F32), pv_ref[...],
                             pg_ref[...], [jnp.zeros((HEAD_W, HEAD_W), F32)] * N_HEADS, tril)
        for h in range(N_HEADS):
            st_ref[h] = sts[h]

    def body(c, carry):
        rows = pl.ds(pl.multiple_of(c * CHUNK, CHUNK), CHUNK)
        outs, sts = _hgrn_chunk(q_ref[rows, :].astype(F32), k_ref[rows, :].astype(F32),
                                v_ref[rows, :], g_ref[rows, :],
                                [st_ref[h] for h in range(N_HEADS)], tril)
        for h in range(N_HEADS):
            st_ref[h] = sts[h]
            o = outs[h]
            cols = slice(h * HEAD_W, (h + 1) * HEAD_W)
            ms = jnp.mean(o * o, axis=-1, keepdims=True)
            on = o * lax.rsqrt(ms + EPS) * gain
            z = z_ref[rows, cols].astype(F32)
            o_ref[rows, cols] = (on * (z * _sigmoid(z))).astype(BF16)
        return carry

    lax.fori_loop(0, n_chunks, body, 0)


def _hgrn(proj, glog, pproj, pglog, gain, *, batch, seq, tile):
    tiles_per_seq = seq // tile

    def seg_spec(seg):
        return pl.BlockSpec((tile, SEG_W), lambda b, t: (b * tiles_per_seq + t, seg))

    def pseg_spec(seg):
        return pl.BlockSpec((PREFIX, SEG_W), lambda b, t: (0, seg))

    return pl.pallas_call(
        _hgrn_kernel,
        grid=(batch, tiles_per_seq),
        in_specs=[
            pseg_spec(SEG_HQ), pseg_spec(SEG_HF), pseg_spec(SEG_HI), pseg_spec(0),
            seg_spec(SEG_HQ), seg_spec(SEG_HF), seg_spec(SEG_HI), seg_spec(SEG_HZ), seg_spec(0),
            pl.BlockSpec((1, HEAD_W), lambda b, t: (0, 0)),
        ],
        out_specs=seg_spec(0),
        out_shape=jax.ShapeDtypeStruct((batch * seq, SEG_W), BF16),
        scratch_shapes=[pltpu.VMEM((N_HEADS, HEAD_W, HEAD_W), F32)],
        compiler_params=pltpu.CompilerParams(
            dimension_semantics=("parallel", "arbitrary"), vmem_limit_bytes=VMEM_LIMIT),
        name="hgrn2",
    )(pproj, pproj, pproj, pglog, proj, proj, proj, proj, glog, gain)


def _attn_kernel(lam_ref, pk_ref, pvt_ref, q_ref, k_ref, vt_ref, z_ref, gain_ref, o_ref,
                 acc_ref, qqt_ref, sa_ref, sb_ref, m_ref, l_ref):
    bq, bk = ATT_BQ, ATT_BK
    nsub = bq // HEAD_W
    seq = q_ref.shape[0]
    lam = lam_ref[0, 0]
    gain = gain_ref[...]
    pvt = pvt_ref[0]
    pk = pk_ref[...]

    drow = lax.broadcasted_iota(jnp.int32, (HEAD_W, HEAD_W), 0)
    first = drow < DA_D
    krow = lax.broadcasted_iota(jnp.int32, (bk, 2 * bq), 0)
    qcol = lax.broadcasted_iota(jnp.int32, (bk, 2 * bq), 1)
    causal = krow <= jnp.where(qcol >= bq, qcol - bq, qcol)
    pvalid = lax.broadcasted_iota(jnp.int32, (PREFIX, 2 * bq), 0) >= N_INERT

    def qblock(i, carry):
        r0 = pl.multiple_of(i * bq, bq)
        qts = [q_ref[pl.ds(r0 + t * HEAD_W, HEAD_W), :].astype(F32).T for t in range(nsub)]
        qqt_ref[...] = jnp.concatenate(
            [jnp.where(first, qt, 0.0) for qt in qts] + [jnp.where(first, 0.0, qt) for qt in qts],
            axis=1).astype(BF16)

        def scores(j):
            c0 = pl.multiple_of(j * bk, bk)
            return jnp.dot(k_ref[pl.ds(c0, bk), :], qqt_ref[...],
                           preferred_element_type=F32)

        def update(s, vtb):
            m = m_ref[...]
            m_new = jnp.maximum(m, jnp.max(s, axis=0, keepdims=True))
            alpha = jnp.exp2(m - m_new)
            p = jnp.exp2(s - m_new)
            m_ref[...] = m_new
            l_ref[...] = alpha * l_ref[...] + jnp.sum(p, axis=0, keepdims=True)
            acc_ref[...] = alpha * acc_ref[...] + jnp.dot(vtb, p.astype(BF16),
                                                          preferred_element_type=F32)

        def full_step(j, s_ref):
            update(s_ref[...], vt_ref[j])

        acc_ref[...] = jnp.zeros_like(acc_ref)
        m_ref[...] = jnp.full_like(m_ref, MASK_VALUE)
        l_ref[...] = jnp.zeros_like(l_ref)

        odd = i % 2

        @pl.when(odd == 1)
        def _():
            sb_ref[...] = scores(0)
            sa_ref[...] = scores(1)
            full_step(0, sb_ref)

        @pl.when(odd == 0)
        def _():
            sa_ref[...] = scores(0)

        def pair(t, carry):
            j = odd + 2 * t
            sb_ref[...] = scores(j + 1)
            full_step(j, sa_ref)
            sa_ref[...] = scores(j + 2)
            full_step(j + 1, sb_ref)
            return carry

        lax.fori_loop(0, (i - odd) // 2, pair, 0)

        s_pre = jnp.dot(pk, qqt_ref[...], preferred_element_type=F32)
        s = jnp.concatenate([jnp.where(causal, sa_ref[...], MASK_VALUE),
                             jnp.where(pvalid, s_pre, MASK_VALUE)], axis=0)
        update(s, jnp.concatenate([vt_ref[i], pvt], axis=1))

        out = acc_ref[...] / l_ref[...]
        ot = out[:, :bq] - lam * out[:, bq:]
        for t in range(nsub):
            o = ot[:, t * HEAD_W:(t + 1) * HEAD_W].T
            ms = jnp.mean(o * o, axis=-1, keepdims=True)
            on = o * lax.rsqrt(ms + EPS) * gain * (1.0 - LAM_INIT)
            rows = pl.ds(r0 + t * HEAD_W, HEAD_W)
            z = z_ref[rows, :].astype(F32)
            o_ref[rows, :] = (on * (z * _sigmoid(z))).astype(BF16)
        return carry

    lax.fori_loop(0, seq // bq, qblock, 0)


def _attn(proj, vt, pproj, pvt, lam, gain, *, batch, seq):
    def seg_spec(seg):
        return pl.BlockSpec((seq, HEAD_W), lambda b, h: (b, seg * N_HEADS + h))

    def pseg_spec(seg):
        return pl.BlockSpec((PREFIX, HEAD_W), lambda b, h: (0, seg * N_HEADS + h))

    return pl.pallas_call(
        _attn_kernel,
        grid=(batch, N_HEADS),
        in_specs=[
            pl.BlockSpec(memory_space=pltpu.SMEM),
            pseg_spec(SEG_AK),
            pl.BlockSpec((1, HEAD_W, PREFIX), lambda b, h: (0, h, 0)),
            seg_spec(SEG_AQ), seg_spec(SEG_AK),
            pl.BlockSpec((seq // ATT_BK, HEAD_W, ATT_BK), lambda b, h: (b, h, 0)),
            seg_spec(SEG_AZ),
            pl.BlockSpec((1, HEAD_W), lambda b, h: (0, 0)),
        ],
        out_specs=pl.BlockSpec((seq, HEAD_W), lambda b, h: (b, h)),
        out_shape=jax.ShapeDtypeStruct((batch * seq, SEG_W), BF16),
        scratch_shapes=[pltpu.VMEM((HEAD_W, 2 * ATT_BQ), F32),
                        pltpu.VMEM((HEAD_W, 2 * ATT_BQ), BF16),
                        pltpu.VMEM((ATT_BK, 2 * ATT_BQ), F32),
                        pltpu.VMEM((ATT_BK, 2 * ATT_BQ), F32),
                        pltpu.VMEM((1, 2 * ATT_BQ), F32),
                        pltpu.VMEM((1, 2 * ATT_BQ), F32)],
        compiler_params=pltpu.CompilerParams(
            dimension_semantics=("parallel", "parallel"), vmem_limit_bytes=VMEM_LIMIT),
        name="diff_attn",
    )(lam, pproj, pvt, proj, proj, vt, proj, gain)


def _out_kernel(x_ref, oa_ref, ob_ref, ga_ref, gb_ref, wa_ref, wb_ref, wo_ref, fg_ref, out_ref):
    ya = jnp.dot(oa_ref[...], wa_ref[...], preferred_element_type=F32)
    yb = jnp.dot(ob_ref[...], wb_ref[...], preferred_element_type=F32)
    m = _sigmoid(ga_ref[...].astype(F32)) * ya + _sigmoid(gb_ref[...].astype(F32)) * yb
    h = x_ref[...] + jnp.dot(m.astype(BF16), wo_ref[...], preferred_element_type=F32)
    ms = jnp.mean(h * h, axis=-1, keepdims=True)
    out_ref[...] = h * lax.rsqrt(ms + EPS) * fg_ref[...]


def _out_stage(x2d, oa, ob, proj, wa, wb, wo, fg, *, tm):
    n_rows = x2d.shape[0]
    tile = lambda col: pl.BlockSpec((tm, SEG_W), lambda i: (i, col))
    wspec = pl.BlockSpec((SEG_W, D_MODEL), lambda i: (0, 0))
    return pl.pallas_call(
        _out_kernel,
        grid=(n_rows // tm,),
        in_specs=[tile(0), tile(0), tile(0), tile(SEG_GA), tile(SEG_GB), wspec, wspec, wspec,
                  pl.BlockSpec((1, D_MODEL), lambda i: (0, 0))],
        out_specs=tile(0),
        out_shape=jax.ShapeDtypeStruct((n_rows, D_MODEL), F32),
        compiler_params=pltpu.CompilerParams(
            dimension_semantics=("parallel",), vmem_limit_bytes=VMEM_LIMIT),
        name="out_stage",
    )(x2d, oa, ob, proj, proj, wa, wb, wo, fg)


def _rope_tables(pos):
    inv = ROPE_THETA ** (-jnp.arange(ROPE_HALF, dtype=F32) * 2.0 / ROPE_DIM)
    ang = pos.astype(F32)[:, None] * inv[None, :]
    cos, sin = jnp.cos(ang), jnp.sin(ang)
    n = pos.shape[0]
    ones = jnp.ones((n, DA_D - ROPE_DIM), F32)
    zeros8 = jnp.zeros((n, ROPE_HALF), F32)
    zeros = jnp.zeros((n, DA_D - ROPE_DIM), F32)
    cos_c = jnp.concatenate([cos, cos, ones], axis=1)
    sa_c = jnp.concatenate([zeros8, sin, zeros], axis=1)
    sb_c = jnp.concatenate([-sin, zeros8, zeros], axis=1)
    rep = lambda t: jnp.concatenate([t, t], axis=1)
    return rep(cos_c), rep(sa_c), rep(sb_c)


def kernel(x, meta_tokens, norm_g, w_in, hg_lb_logits, hg_norm_g, da_lambda, da_norm_g,
           w_branch_a, w_branch_b, w_out, final_g):
    batch, seq, d = x.shape
    assert d == D_MODEL and seq % ATT_BQ == 0 and seq % 1024 == 0
    assert norm_g.shape[0] == 1, "single-layer block"

    x2d = x.reshape(batch * seq, d)
    prefix = jnp.concatenate([jnp.zeros((N_INERT, d), x.dtype), meta_tokens.astype(x.dtype)], axis=0)

    lb = jnp.cumsum(jax.nn.softmax(hg_lb_logits.astype(F32), axis=0), axis=0)[0][None, :]
    lp = da_lambda[0].astype(F32)
    lam = (jnp.exp(jnp.sum(lp[0] * lp[1])) - jnp.exp(jnp.sum(lp[2] * lp[3])) + LAM_INIT).reshape(1, 1)

    w = w_in[0]
    seg = lambda r: w[:, r * SEG_W:(r + 1) * SEG_W]
    w_ext = jnp.concatenate([seg(r) for r in REF_SEG_ORDER] + [seg(REF_SEG_AV).T], axis=1).astype(BF16)
    ng = norm_g[0][None, :].astype(F32)
    tabs_main = _rope_tables(jnp.arange(seq) + N_META)
    tabs_pre = _rope_tables(jnp.maximum(jnp.arange(PREFIX) - N_INERT, 0))

    proj, glog, vt = _inproj(x2d, ng, w_ext, lb, *tabs_main, tm=1024, rows_per_seq=seq, n_inert=0)
    pproj, pglog, pvt = _inproj(prefix, ng, w_ext, lb, *tabs_pre, tm=PREFIX, rows_per_seq=PREFIX,
                                n_inert=N_INERT)

    oa = _hgrn(proj, glog, pproj, pglog, hg_norm_g[0][None, :].astype(F32), batch=batch, seq=seq,
               tile=512)
    ob = _attn(proj, vt, pproj, pvt, lam, da_norm_g[0][None, :].astype(F32), batch=batch, seq=seq)

    out = _out_stage(x2d, oa, ob, proj, w_branch_a[0].astype(BF16), w_branch_b[0].astype(BF16),
                     w_out[0].astype(BF16), final_g[None, :].astype(F32), tm=512)
    return out.reshape(batch, seq, d)
```

```python
import functools

import jax
import jax.numpy as jnp
import numpy as np
from jax import lax
from jax.experimental import pallas as pl
from jax.experimental.pallas import tpu as pltpu

F32 = jnp.float32
BF16 = jnp.bfloat16

D_MODEL = 1024
N_HEADS = 8
HEAD_W = 128
DA_D = 64
N_META = 16
PREFIX = 128
N_INERT = PREFIX - N_META
ROPE_DIM = 16
ROPE_HALF = ROPE_DIM // 2
ROPE_THETA = 500000.0
EPS = 1e-6
LAM_INIT = 0.2
MASK_VALUE = -1e30
Q_SCALE = DA_D ** -0.5 * float(np.log2(np.e))

SEG_W = 1024
N_SEG = 9
SEG_HQ, SEG_HF, SEG_HI, SEG_HZ, SEG_AQ, SEG_AK, SEG_AZ, SEG_GA, SEG_GB = range(N_SEG)
REF_SEG_ORDER = (0, 1, 2, 3, 4, 5, 7, 8, 9)
REF_SEG_AV = 6

INPROJ_ROWS = 256
CHUNK = 128
SUB = 16
N_SUB = CHUNK // SUB
ATT_BQ = 512
ATT_BK = 512
ATT_STRIP = 256

VMEM_LIMIT = 56 * 1024 * 1024


def _sigmoid(x):
    return 1.0 / (1.0 + jnp.exp(-x))


def _inproj_kernel(n_inert, x_ref, ng_ref, w_ref, lb_ref, cos_ref, sa_ref, sb_ref,
                   out_ref, glog_ref, vt_ref, u_ref):
    j = pl.program_id(1)

    @pl.when(j == 0)
    def _():
        x = x_ref[...]
        ms = jnp.mean(x * x, axis=-1, keepdims=True)
        u_ref[...] = (x * lax.rsqrt(ms + EPS) * ng_ref[...]).astype(BF16)

    is_rope = jnp.logical_or(j == SEG_AQ, j == SEG_AK)
    is_gate = j == SEG_HF
    is_vt = j == N_SEG

    tm = u_ref.shape[0]
    rb = min(tm, INPROJ_ROWS)

    def project(rows=slice(None)):
        return jnp.dot(u_ref[rows, :], w_ref[...], preferred_element_type=F32)

    @pl.when(is_gate)
    def _():
        lb = lb_ref[...]
        for r in range(tm // rb):
            rows = slice(r * rb, (r + 1) * rb)
            f = lb + (1.0 - lb) * _sigmoid(project(rows))
            if n_inert:
                ridx = lax.broadcasted_iota(jnp.int32, f.shape, 0) + r * rb
                f = jnp.where(ridx >= n_inert, f, 1.0)
            glog_ref[rows, :] = jnp.log(f)
            out_ref[rows, :] = (1.0 - f).astype(BF16)

    @pl.when(is_rope)
    def _():
        scale = jnp.where(j == SEG_AQ, Q_SCALE, 1.0).astype(F32)
        for r in range(tm // rb):
            rows = slice(r * rb, (r + 1) * rb)
            acc = project(rows)
            cos = cos_ref[rows, :]
            sa = sa_ref[rows, :]
            sb = sb_ref[rows, :]
            for c in range(SEG_W // HEAD_W):
                xs = acc[:, c * HEAD_W:(c + 1) * HEAD_W]
                y = (xs * cos + pltpu.roll(xs, ROPE_HALF, 1) * sa
                     + pltpu.roll(xs, HEAD_W - ROPE_HALF, 1) * sb)
                out_ref[rows, c * HEAD_W:(c + 1) * HEAD_W] = (y * scale).astype(BF16)

    @pl.when(is_vt)
    def _():
        acc_t = lax.dot_general(w_ref[...], u_ref[...], (((1,), (1,)), ((), ())),
                                preferred_element_type=F32)
        kb = vt_ref.shape[2]
        for c in range(vt_ref.shape[0]):
            vt_ref[c] = acc_t[:, c * kb:(c + 1) * kb].astype(BF16)

    @pl.when(jnp.logical_not(jnp.logical_or(jnp.logical_or(is_rope, is_gate), is_vt)))
    def _():
        out_ref[...] = project().astype(BF16)


def _inproj(x2d, norm_g, w_ext, lb, cos_t, sa_t, sb_t, *, tm, rows_per_seq, n_inert):
    n_rows = x2d.shape[0]
    n_tiles = n_rows // tm
    tiles_per_seq = rows_per_seq // tm
    kb = min(tm, ATT_BK)
    row_spec = pl.BlockSpec((1, D_MODEL), lambda i, j: (0, 0))
    tab_spec = pl.BlockSpec((tm, HEAD_W), lambda i, j: (i % tiles_per_seq, 0))
    return pl.pallas_call(
        functools.partial(_inproj_kernel, n_inert),
        grid=(n_tiles, N_SEG + 1),
        in_specs=[
            pl.BlockSpec((tm, D_MODEL), lambda i, j: (i, 0)),
            row_spec,
            pl.BlockSpec((D_MODEL, SEG_W), lambda i, j: (0, j)),
            row_spec,
            tab_spec, tab_spec, tab_spec,
        ],
        out_specs=[
            pl.BlockSpec((tm, SEG_W), lambda i, j: (i, jnp.minimum(j, N_SEG - 1))),
            pl.BlockSpec((tm, SEG_W), lambda i, j: (i, 0)),
            pl.BlockSpec((tm // kb, D_MODEL, kb), lambda i, j: (i, 0, 0)),
        ],
        out_shape=[
            jax.ShapeDtypeStruct((n_rows, N_SEG * SEG_W), BF16),
            jax.ShapeDtypeStruct((n_rows, SEG_W), F32),
            jax.ShapeDtypeStruct((n_rows // kb, D_MODEL, kb), BF16),
        ],
        scratch_shapes=[pltpu.VMEM((tm, D_MODEL), BF16)],
        compiler_params=pltpu.CompilerParams(
            dimension_semantics=("parallel", "arbitrary"), vmem_limit_bytes=VMEM_LIMIT),
        name="inproj",
    )(x2d, norm_g, w_ext, lb, cos_t, sa_t, sb_t)


def _bcast_rows(rows, n):
    return jnp.concatenate([jnp.broadcast_to(r, (n, r.shape[1])) for r in rows], axis=0)


def _hgrn_chunk(q, k, v, g, sts, tril):
    n_heads = len(sts)
    width = q.shape[1]
    head = lambda x, h: x[:, h * HEAD_W:(h + 1) * HEAD_W]
    nt = (((1,), (1,)), ((), ()))

    g_hi = g.astype(BF16)
    r1 = g - g_hi.astype(F32)
    g_mid = r1.astype(BF16)
    g_lo = (r1 - g_mid.astype(F32)).astype(BF16)
    bs = jnp.dot(tril.astype(BF16), jnp.concatenate([g_hi, g_mid, g_lo], axis=1),
                 preferred_element_type=F32)
    b = bs[:, :width] + bs[:, width:2 * width] + bs[:, 2 * width:]

    ends = [b[SUB * s + SUB - 1:SUB * s + SUB, :] for s in range(N_SUB)]
    starts = [jnp.zeros_like(ends[0])] + ends[:-1]
    end_b = _bcast_rows(ends, SUB)
    start_b = _bcast_rows(starts, SUB)
    b_last = ends[-1]

    qd = q * jnp.exp(b - start_b)
    khat = k * jnp.exp(end_b - b)
    q_s = (qd * _bcast_rows([jnp.exp(s) for s in starts], SUB)).astype(BF16)
    k_s = (khat * _bcast_rows([jnp.exp(b_last - e) for e in ends], SUB)).astype(BF16)
    decay = jnp.exp(b_last)

    o_inter = [lax.dot_general(head(q_s, h), sts[h].astype(BF16), nt, preferred_element_type=F32)
               for h in range(n_heads)]
    st_new = [sts[h] * head(decay, h)
              + lax.dot_general(head(v, h), head(k_s, h), (((0,), (0,)), ((), ())),
                                preferred_element_type=F32)
              for h in range(n_heads)]

    qd_bf = qd.astype(BF16)
    a_rows = [[] for _ in range(n_heads)]
    for i in range(N_SUB):
        parts = []
        for jj in range(i + 1):
            kj = khat[SUB * jj:SUB * jj + SUB, :]
            if jj == i:
                w = jnp.exp(starts[i] - ends[i])
            elif jj == i - 1:
                w = None
            else:
                w = jnp.exp(starts[i] - ends[jj])
            parts.append(kj if w is None else kj * w)
        if i + 1 < N_SUB:
            parts.append(jnp.zeros((CHUNK - SUB * (i + 1), width), F32))
        km = jnp.concatenate(parts, axis=0).astype(BF16)
        qi = qd_bf[SUB * i:SUB * i + SUB, :]
        for h in range(n_heads):
            a_rows[h].append(lax.dot_general(head(qi, h), head(km, h), nt,
                                             preferred_element_type=F32))
    outs = []
    for h in range(n_heads):
        a = jnp.where(tril, jnp.concatenate(a_rows[h], axis=0), 0.0).astype(BF16)
        outs.append(jnp.dot(a, head(v, h), preferred_element_type=F32) + o_inter[h])
    return outs, st_new


def _hgrn_kernel(pq_ref, pk_ref, pv_ref, pg_ref, q_ref, k_ref, v_ref, z_ref, g_ref, gain_ref, o_ref,
                 st_ref):
    n_chunks = q_ref.shape[0] // CHUNK
    tril = (lax.broadcasted_iota(jnp.int32, (CHUNK, CHUNK), 0)
            >= lax.broadcasted_iota(jnp.int32, (CHUNK, CHUNK), 1))
    gain = gain_ref[...]

    @pl.when(pl.program_id(1) == 0)
    def _():
        _, sts = _hgrn_chunk(pq_ref[...].astype(F32), pk_ref[...].astype(F32), pv_ref[...],
                             pg_ref[...], [jnp.zeros((HEAD_W, HEAD_W), F32)] * N_HEADS, tril)
        for h in range(N_HEADS):
            st_ref[h] = sts[h]

    def body(c, carry):
        rows = pl.ds(pl.multiple_of(c * CHUNK, CHUNK), CHUNK)
        outs, sts = _hgrn_chunk(q_ref[rows, :].astype(F32), k_ref[rows, :].astype(F32),
                                v_ref[rows, :], g_ref[rows, :],
                                [st_ref[h] for h in range(N_HEADS)], tril)
        for h in range(N_HEADS):
            st_ref[h] = sts[h]
            o = outs[h]
            cols = slice(h * HEAD_W, (h + 1) * HEAD_W)
            ms = jnp.mean(o * o, axis=-1, keepdims=True)
            on = o * lax.rsqrt(ms + EPS) * gain
            z = z_ref[rows, cols].astype(F32)
            o_ref[rows, cols] = (on * (z * _sigmoid(z))).astype(BF16)
        return carry

    lax.fori_loop(0, n_chunks, body, 0)


def _hgrn(proj, glog, pproj, pglog, gain, *, batch, seq, tile):
    tiles_per_seq = seq // tile

    def seg_spec(seg):
        return pl.BlockSpec((tile, SEG_W), lambda b, t: (b * tiles_per_seq + t, seg))

    def pseg_spec(seg):
        return pl.BlockSpec((PREFIX, SEG_W), lambda b, t: (0, seg))

    return pl.pallas_call(
        _hgrn_kernel,
        grid=(batch, tiles_per_seq),
        in_specs=[
            pseg_spec(SEG_HQ), pseg_spec(SEG_HF), pseg_spec(SEG_HI), pseg_spec(0),
            seg_spec(SEG_HQ), seg_spec(SEG_HF), seg_spec(SEG_HI), seg_spec(SEG_HZ), seg_spec(0),
            pl.BlockSpec((1, HEAD_W), lambda b, t: (0, 0)),
        ],
        out_specs=seg_spec(0),
        out_shape=jax.ShapeDtypeStruct((batch * seq, SEG_W), BF16),
        scratch_shapes=[pltpu.VMEM((N_HEADS, HEAD_W, HEAD_W), F32)],
        compiler_params=pltpu.CompilerParams(
            dimension_semantics=("parallel", "arbitrary"), vmem_limit_bytes=VMEM_LIMIT),
        name="hgrn2",
    )(pproj, pproj, pproj, pglog, proj, proj, proj, proj, glog, gain)


def _attn_kernel(lam_ref, pk_ref, pvt_ref, q_ref, k_ref, vt_ref, z_ref, gaint_ref, o_ref,
                 acc_ref, qqt_ref, s_ref, m_ref, l_ref):
    bq, bk = ATT_BQ, ATT_BK
    nsub = bq // HEAD_W
    n_q = q_ref.shape[0] // bq
    lam = lam_ref[0, 0]
    steps = [(i, j) for i in range(n_q) for j in range(i + 1)]

    drow = lax.broadcasted_iota(jnp.int32, (HEAD_W, HEAD_W), 0)
    first = drow < DA_D
    pvalid = lax.broadcasted_iota(jnp.int32, (PREFIX, ATT_STRIP), 0) >= N_INERT

    def causal(c):
        kr = key_rows(c)
        krow = lax.broadcasted_iota(jnp.int32, (kr, ATT_STRIP), 0)
        query = lax.broadcasted_iota(jnp.int32, (kr, ATT_STRIP), 1) + (c * ATT_STRIP) % bq
        return krow <= query

    def key_rows(c):
        return min(bk, (c * ATT_STRIP) % bq + ATT_STRIP)

    def build_q(i):
        qts = [q_ref[i * bq + t * HEAD_W:i * bq + (t + 1) * HEAD_W, :].astype(F32).T
               for t in range(nsub)]
        qqt_ref[i % 2] = jnp.concatenate(
            [jnp.where(first, qt, 0.0) for qt in qts] + [jnp.where(first, 0.0, qt) for qt in qts],
            axis=1).astype(BF16)

    def issue_scores(t, c):
        i, j = steps[t]
        cols = slice(c * ATT_STRIP, (c + 1) * ATT_STRIP)
        qs = qqt_ref[i % 2, :, cols]
        kr = key_rows(c) if j == i else bk
        if j == 0:
            keys = jnp.concatenate([pk_ref[...], k_ref[0:kr, :]], axis=0)
            s_ref[t % 2, 0:PREFIX + kr, cols] = jnp.dot(keys, qs, preferred_element_type=F32)
        else:
            s_ref[t % 2, 0:kr, cols] = jnp.dot(k_ref[j * bk:j * bk + kr, :], qs,
                                               preferred_element_type=F32)

    def update(t, c):
        i, j = steps[t]
        par = i % 2
        cols = slice(c * ATT_STRIP, (c + 1) * ATT_STRIP)
        kr = key_rows(c) if j == i else bk
        if j == 0:
            s = s_ref[t % 2, 0:PREFIX + kr, cols]
            body = s[PREFIX:, :]
            if i == 0:
                body = jnp.where(causal(c), body, MASK_VALUE)
            s = jnp.concatenate([jnp.where(pvalid, s[0:PREFIX, :], MASK_VALUE), body], axis=0)
            vtb = jnp.concatenate([pvt_ref[0], vt_ref[0, :, 0:kr]], axis=1)
            m_new = jnp.max(s, axis=0, keepdims=True)
            p = jnp.exp2(s - m_new)
            m_ref[par, :, cols] = m_new
            l_ref[par, :, cols] = jnp.sum(p, axis=0, keepdims=True)
            acc_ref[par, :, cols] = jnp.dot(vtb, p.astype(BF16), preferred_element_type=F32)
        else:
            s = s_ref[t % 2, 0:kr, cols]
            if j == i:
                s = jnp.where(causal(c), s, MASK_VALUE)
            m = m_ref[par, :, cols]
            m_new = jnp.maximum(m, jnp.max(s, axis=0, keepdims=True))
            alpha = jnp.exp2(m - m_new)
            p = jnp.exp2(s - m_new)
            m_ref[par, :, cols] = m_new
            l_ref[par, :, cols] = alpha * l_ref[par, :, cols] + jnp.sum(p, axis=0, keepdims=True)
            acc_ref[par, :, cols] = alpha * acc_ref[par, :, cols] + jnp.dot(
                vt_ref[j, :, 0:kr], p.astype(BF16), preferred_element_type=F32)

    def epilogue(i):
        par = i % 2
        out = acc_ref[par] * (1.0 / l_ref[par])
        ot = out[:, :bq] - lam * out[:, bq:]
        ms = jnp.mean(ot * ot, axis=0, keepdims=True)
        ot = ot * (lax.rsqrt(ms + EPS) * (1.0 - LAM_INIT))
        for t in range(nsub):
            cols = slice(t * HEAD_W, (t + 1) * HEAD_W)
            o = (ot[:, cols] * gaint_ref[...]).T
            rows = slice(i * bq + t * HEAD_W, i * bq + (t + 1) * HEAD_W)
            z = z_ref[rows, :].astype(F32)
            o_ref[rows, :] = (o * (z * _sigmoid(z))).astype(BF16)

    n_strips = 2 * bq // ATT_STRIP
    build_q(0)
    for c in range(n_strips):
        issue_scores(0, c)
    for t, (i, j) in enumerate(steps):
        has_next = t + 1 < len(steps)
        if has_next and steps[t + 1][1] == 0:
            build_q(steps[t + 1][0])
        for c in range(n_strips):
            if has_next:
                issue_scores(t + 1, c)
            update(t, c)
        if j == i:
            epilogue(i)


def _attn(proj, vt, pproj, pvt, lam, gain, *, batch, seq):
    def seg_spec(seg):
        return pl.BlockSpec((seq, HEAD_W), lambda b, h: (b, seg * N_HEADS + h))

    def pseg_spec(seg):
        return pl.BlockSpec((PREFIX, HEAD_W), lambda b, h: (0, seg * N_HEADS + h))

    return pl.pallas_call(
        _attn_kernel,
        grid=(batch, N_HEADS),
        in_specs=[
            pl.BlockSpec(memory_space=pltpu.SMEM),
            pseg_spec(SEG_AK),
            pl.BlockSpec((1, HEAD_W, PREFIX), lambda b, h: (0, h, 0)),
            seg_spec(SEG_AQ), seg_spec(SEG_AK),
            pl.BlockSpec((seq // ATT_BK, HEAD_W, ATT_BK), lambda b, h: (b, h, 0)),
            seg_spec(SEG_AZ),
            pl.BlockSpec((HEAD_W, HEAD_W), lambda b, h: (0, 0)),
        ],
        out_specs=pl.BlockSpec((seq, HEAD_W), lambda b, h: (b, h)),
        out_shape=jax.ShapeDtypeStruct((batch * seq, SEG_W), BF16),
        scratch_shapes=[pltpu.VMEM((2, HEAD_W, 2 * ATT_BQ), F32),
                        pltpu.VMEM((2, HEAD_W, 2 * ATT_BQ), BF16),
                        pltpu.VMEM((2, ATT_BK + PREFIX, 2 * ATT_BQ), F32),
                        pltpu.VMEM((2, 1, 2 * ATT_BQ), F32),
                        pltpu.VMEM((2, 1, 2 * ATT_BQ), F32)],
        compiler_params=pltpu.CompilerParams(
            dimension_semantics=("parallel", "parallel"), vmem_limit_bytes=VMEM_LIMIT),
        name="diff_attn",
    )(lam, pproj, pvt, proj, proj, vt, proj, gain)


def _out_kernel(x_ref, oa_ref, ob_ref, ga_ref, gb_ref, wa_ref, wb_ref, wo_ref, fg_ref, out_ref):
    ya = jnp.dot(oa_ref[...], wa_ref[...], preferred_element_type=F32)
    yb = jnp.dot(ob_ref[...], wb_ref[...], preferred_element_type=F32)
    m = _sigmoid(ga_ref[...].astype(F32)) * ya + _sigmoid(gb_ref[...].astype(F32)) * yb
    h = x_ref[...] + jnp.dot(m.astype(BF16), wo_ref[...], preferred_element_type=F32)
    ms = jnp.mean(h * h, axis=-1, keepdims=True)
    out_ref[...] = h * lax.rsqrt(ms + EPS) * fg_ref[...]


def _out_stage(x2d, oa, ob, proj, wa, wb, wo, fg, *, tm):
    n_rows = x2d.shape[0]
    tile = lambda col: pl.BlockSpec((tm, SEG_W), lambda i: (i, col))
    wspec = pl.BlockSpec((SEG_W, D_MODEL), lambda i: (0, 0))
    return pl.pallas_call(
        _out_kernel,
        grid=(n_rows // tm,),
        in_specs=[tile(0), tile(0), tile(0), tile(SEG_GA), tile(SEG_GB), wspec, wspec, wspec,
                  pl.BlockSpec((1, D_MODEL), lambda i: (0, 0))],
        out_specs=tile(0),
        out_shape=jax.ShapeDtypeStruct((n_rows, D_MODEL), F32),
        compiler_params=pltpu.CompilerParams(
            dimension_semantics=("parallel",), vmem_limit_bytes=VMEM_LIMIT),
        name="out_stage",
    )(x2d, oa, ob, proj, proj, wa, wb, wo, fg)


def _rope_tables(pos):
    inv = ROPE_THETA ** (-jnp.arange(ROPE_HALF, dtype=F32) * 2.0 / ROPE_DIM)
    ang = pos.astype(F32)[:, None] * inv[None, :]
    cos, sin = jnp.cos(ang), jnp.sin(ang)
    n = pos.shape[0]
    ones = jnp.ones((n, DA_D - ROPE_DIM), F32)
    zeros8 = jnp.zeros((n, ROPE_HALF), F32)
    zeros = jnp.zeros((n, DA_D - ROPE_DIM), F32)
    cos_c = jnp.concatenate([cos, cos, ones], axis=1)
    sa_c = jnp.concatenate([zeros8, sin, zeros], axis=1)
    sb_c = jnp.concatenate([-sin, zeros8, zeros], axis=1)
    rep = lambda t: jnp.concatenate([t, t], axis=1)
    return rep(cos_c), rep(sa_c), rep(sb_c)


def kernel(x, meta_tokens, norm_g, w_in, hg_lb_logits, hg_norm_g, da_lambda, da_norm_g,
           w_branch_a, w_branch_b, w_out, final_g):
    batch, seq, d = x.shape
    assert d == D_MODEL and seq % ATT_BQ == 0 and seq % 1024 == 0
    assert norm_g.shape[0] == 1, "single-layer block"

    x2d = x.reshape(batch * seq, d)
    prefix = jnp.concatenate([jnp.zeros((N_INERT, d), x.dtype), meta_tokens.astype(x.dtype)], axis=0)

    lb = jnp.cumsum(jax.nn.softmax(hg_lb_logits.astype(F32), axis=0), axis=0)[0][None, :]
    lp = da_lambda[0].astype(F32)
    lam = (jnp.exp(jnp.sum(lp[0] * lp[1])) - jnp.exp(jnp.sum(lp[2] * lp[3])) + LAM_INIT).reshape(1, 1)

    w = w_in[0]
    seg = lambda r: w[:, r * SEG_W:(r + 1) * SEG_W]
    w_ext = jnp.concatenate([seg(r) for r in REF_SEG_ORDER] + [seg(REF_SEG_AV).T], axis=1).astype(BF16)
    ng = norm_g[0][None, :].astype(F32)
    tabs_main = _rope_tables(jnp.arange(seq) + N_META)
    tabs_pre = _rope_tables(jnp.maximum(jnp.arange(PREFIX) - N_INERT, 0))

    proj, glog, vt = _inproj(x2d, ng, w_ext, lb, *tabs_main, tm=1024, rows_per_seq=seq, n_inert=0)
    pproj, pglog, pvt = _inproj(prefix, ng, w_ext, lb, *tabs_pre, tm=PREFIX, rows_per_seq=PREFIX,
                                n_inert=N_INERT)

    oa = _hgrn(proj, glog, pproj, pglog, hg_norm_g[0][None, :].astype(F32), batch=batch, seq=seq,
               tile=512)
    da_gain_rows = jnp.broadcast_to(da_norm_g[0].astype(F32)[:, None], (HEAD_W, HEAD_W))
    ob = _attn(proj, vt, pproj, pvt, lam, da_gain_rows, batch=batch, seq=seq)

    out = _out_stage(x2d, oa, ob, proj, w_branch_a[0].astype(BF16), w_branch_b[0].astype(BF16),
                     w_out[0].astype(BF16), final_g[None, :].astype(F32), tm=512)
    return out.reshape(batch, seq, d)
```

```python
import functools

import jax
import jax.numpy as jnp
import numpy as np
from jax import lax
from jax.experimental import pallas as pl
from jax.experimental.pallas import tpu as pltpu

F32 = jnp.float32
BF16 = jnp.bfloat16

D_MODEL = 1024
N_HEADS = 8
HEAD_W = 128
DA_D = 64
N_META = 16
PREFIX = 128
N_INERT = PREFIX - N_META
ROPE_DIM = 16
ROPE_HALF = ROPE_DIM // 2
ROPE_THETA = 500000.0
EPS = 1e-6
LAM_INIT = 0.2
MASK_VALUE = -1e30
Q_SCALE = DA_D ** -0.5 * float(np.log2(np.e))

SEG_W = 1024
N_SEG = 9
SEG_HQ, SEG_HF, SEG_HI, SEG_HZ, SEG_AQ, SEG_AK, SEG_AZ, SEG_GA, SEG_GB = range(N_SEG)
REF_SEG_AV = 6

INPROJ_ROWS = 256
CHUNK = 128
SUB = 16
N_SUB = CHUNK // SUB
ATT_BQ = 512
ATT_BK = 512
ATT_STRIP = 256

VMEM_LIMIT = 56 * 1024 * 1024


def _sigmoid(x):
    return 1.0 / (1.0 + jnp.exp(-x))


def _inproj_kernel(n_inert, x_ref, ng_ref, w_ref, wvt_ref, lb_ref, cos_ref, sa_ref, sb_ref,
                   out_ref, glog_ref, vt_ref, u_ref):
    j = pl.program_id(1)

    @pl.when(j == 0)
    def _():
        x = x_ref[...]
        ms = jnp.mean(x * x, axis=-1, keepdims=True)
        u_ref[...] = (x * lax.rsqrt(ms + EPS) * ng_ref[...]).astype(BF16)

    is_rope = jnp.logical_or(j == SEG_AQ, j == SEG_AK)
    is_gate = j == SEG_HF
    is_vt = j == N_SEG

    tm = u_ref.shape[0]
    rb = min(tm, INPROJ_ROWS)

    def project(rows=slice(None)):
        ref_seg = jnp.minimum(j + (j >= REF_SEG_AV).astype(jnp.int32), N_SEG)
        return jnp.dot(u_ref[rows, :], w_ref[ref_seg], preferred_element_type=F32)

    @pl.when(is_gate)
    def _():
        lb = lb_ref[...]
        for r in range(tm // rb):
            rows = slice(r * rb, (r + 1) * rb)
            f = lb + (1.0 - lb) * _sigmoid(project(rows))
            if n_inert:
                ridx = lax.broadcasted_iota(jnp.int32, f.shape, 0) + r * rb
                f = jnp.where(ridx >= n_inert, f, 1.0)
            glog_ref[rows, :] = jnp.log2(f)
            out_ref[rows, :] = (1.0 - f).astype(BF16)

    @pl.when(is_rope)
    def _():
        scale = jnp.where(j == SEG_AQ, Q_SCALE, 1.0).astype(F32)
        for r in range(tm // rb):
            rows = slice(r * rb, (r + 1) * rb)
            acc = project(rows)
            cos = cos_ref[rows, :]
            sa = sa_ref[rows, :]
            sb = sb_ref[rows, :]
            for c in range(SEG_W // HEAD_W):
                xs = acc[:, c * HEAD_W:(c + 1) * HEAD_W]
                y = (xs * cos + pltpu.roll(xs, ROPE_HALF, 1) * sa
                     + pltpu.roll(xs, HEAD_W - ROPE_HALF, 1) * sb)
                out_ref[rows, c * HEAD_W:(c + 1) * HEAD_W] = (y * scale).astype(BF16)

    @pl.when(is_vt)
    def _():
        acc_t = lax.dot_general(wvt_ref[...], u_ref[...], (((1,), (1,)), ((), ())),
                                preferred_element_type=F32)
        kb = vt_ref.shape[2]
        for c in range(vt_ref.shape[0]):
            vt_ref[c] = acc_t[:, c * kb:(c + 1) * kb].astype(BF16)

    @pl.when(jnp.logical_not(jnp.logical_or(jnp.logical_or(is_rope, is_gate), is_vt)))
    def _():
        out_ref[...] = project().astype(BF16)


def _inproj(x2d, norm_g, w_bf, w_vt, lb, cos_t, sa_t, sb_t, *, tm, rows_per_seq, n_inert):
    n_rows = x2d.shape[0]
    n_tiles = n_rows // tm
    tiles_per_seq = rows_per_seq // tm
    kb = min(tm, ATT_BK)
    row_spec = pl.BlockSpec((1, D_MODEL), lambda i, j: (0, 0))
    tab_spec = pl.BlockSpec((tm, HEAD_W), lambda i, j: (i % tiles_per_seq, 0))

    return pl.pallas_call(
        functools.partial(_inproj_kernel, n_inert),
        grid=(n_tiles, N_SEG + 1),
        in_specs=[
            pl.BlockSpec((tm, D_MODEL), lambda i, j: (i, 0)),
            row_spec,
            pl.BlockSpec((N_SEG + 1, D_MODEL, SEG_W), lambda i, j: (0, 0, 0),
                         pipeline_mode=pl.Buffered(1)),
            pl.BlockSpec((SEG_W, D_MODEL), lambda i, j: (0, 0)),
            row_spec,
            tab_spec, tab_spec, tab_spec,
        ],
        out_specs=[
            pl.BlockSpec((tm, SEG_W), lambda i, j: (i, jnp.minimum(j, N_SEG - 1))),
            pl.BlockSpec((tm, SEG_W), lambda i, j: (i, 0)),
            pl.BlockSpec((tm // kb, D_MODEL, kb), lambda i, j: (i, 0, 0)),
        ],
        out_shape=[
            jax.ShapeDtypeStruct((n_rows, N_SEG * SEG_W), BF16),
            jax.ShapeDtypeStruct((n_rows, SEG_W), F32),
            jax.ShapeDtypeStruct((n_rows // kb, D_MODEL, kb), BF16),
        ],
        scratch_shapes=[pltpu.VMEM((tm, D_MODEL), BF16)],
        compiler_params=pltpu.CompilerParams(
            dimension_semantics=("parallel", "arbitrary"), vmem_limit_bytes=VMEM_LIMIT),
        name="inproj",
    )(x2d, norm_g, w_bf, w_vt, lb, cos_t, sa_t, sb_t)


def _bcast_rows(rows, n):
    return jnp.concatenate([jnp.broadcast_to(r, (n, r.shape[1])) for r in rows], axis=0)


def _hgrn_chunk(q, k, v, g, sts, tril):
    n_heads = len(sts)
    width = q.shape[1]
    head = lambda x, h: x[:, h * HEAD_W:(h + 1) * HEAD_W]
    nt = (((1,), (1,)), ((), ()))

    g_hi = g.astype(BF16)
    g_lo = (g - g_hi.astype(F32)).astype(BF16)
    bs = jnp.dot(tril.astype(BF16), jnp.concatenate([g_hi, g_lo], axis=1),
                 preferred_element_type=F32)
    b = bs[:, :width] + bs[:, width:]

    ends = [b[SUB * s + SUB - 1:SUB * s + SUB, :] for s in range(N_SUB)]
    starts = [jnp.zeros_like(ends[0])] + ends[:-1]
    end_b = _bcast_rows(ends, SUB)
    start_b = _bcast_rows(starts, SUB)
    b_last = ends[-1]

    qd = q * jnp.exp2(b - start_b)
    khat = k * jnp.exp2(end_b - b)
    qd_bf = qd.astype(BF16)
    khat_bf = khat.astype(BF16)
    q_s = qd_bf * _bcast_rows([jnp.exp2(s) for s in starts], SUB).astype(BF16)
    k_s = khat_bf * _bcast_rows([jnp.exp2(b_last - e) for e in ends], SUB).astype(BF16)
    decay = jnp.exp2(b_last)

    o_inter = [lax.dot_general(head(q_s, h), sts[h].astype(BF16), nt, preferred_element_type=F32)
               for h in range(n_heads)]
    st_new = [sts[h] * head(decay, h)
              + lax.dot_general(head(v, h), head(k_s, h), (((0,), (0,)), ((), ())),
                                preferred_element_type=F32)
              for h in range(n_heads)]

    a_rows = [[] for _ in range(n_heads)]
    for i in range(N_SUB):
        parts = []
        for jj in range(i + 1):
            kj = khat_bf[SUB * jj:SUB * jj + SUB, :]
            if jj == i:
                w = jnp.exp2(starts[i] - ends[i])
            elif jj == i - 1:
                w = None
            else:
                w = jnp.exp2(starts[i] - ends[jj])
            parts.append(kj if w is None else kj * jnp.broadcast_to(w, kj.shape).astype(BF16))
        if i + 1 < N_SUB:
            parts.append(jnp.zeros((CHUNK - SUB * (i + 1), width), BF16))
        km = jnp.concatenate(parts, axis=0)
        qi = qd_bf[SUB * i:SUB * i + SUB, :]
        for h in range(n_heads):
            a_rows[h].append(lax.dot_general(head(qi, h), head(km, h), nt,
                                             preferred_element_type=F32))
    outs = []
    for h in range(n_heads):
        a = jnp.where(tril, jnp.concatenate(a_rows[h], axis=0), 0.0).astype(BF16)
        outs.append(jnp.dot(a, head(v, h), preferred_element_type=F32) + o_inter[h])
    return outs, st_new


def _hgrn_kernel(pq_ref, pk_ref, pv_ref, pg_ref, q_ref, k_ref, v_ref, z_ref, g_ref, gain_ref, o_ref,
                 st_ref):
    n_chunks = q_ref.shape[0] // CHUNK
    tril = (lax.broadcasted_iota(jnp.int32, (CHUNK, CHUNK), 0)
            >= lax.broadcasted_iota(jnp.int32, (CHUNK, CHUNK), 1))
    gain = gain_ref[...]

    @pl.when(pl.program_id(1) == 0)
    def _():
        _, sts = _hgrn_chunk(pq_ref[...].astype(F32), pk_ref[...].astype(F32), pv_ref[...],
                             pg_ref[...], [jnp.zeros((HEAD_W, HEAD_W), F32)] * N_HEADS, tril)
        for h in range(N_HEADS):
            st_ref[h] = sts[h]

    sts = [st_ref[h] for h in range(N_HEADS)]
    for c in range(n_chunks):
        rows = slice(c * CHUNK, (c + 1) * CHUNK)
        outs, sts = _hgrn_chunk(q_ref[rows, :].astype(F32), k_ref[rows, :].astype(F32),
                                v_ref[rows, :], g_ref[rows, :], sts, tril)
        for h in range(N_HEADS):
            o = outs[h]
            cols = slice(h * HEAD_W, (h + 1) * HEAD_W)
            ms = jnp.mean(o * o, axis=-1, keepdims=True)
            on = o * lax.rsqrt(ms + EPS) * gain
            z = z_ref[rows, cols].astype(F32)
            o_ref[rows, cols] = (on * (z * _sigmoid(z))).astype(BF16)
    for h in range(N_HEADS):
        st_ref[h] = sts[h]


def _hgrn(proj, glog, pproj, pglog, gain, *, batch, seq, tile):
    tiles_per_seq = seq // tile

    def seg_spec(seg):
        return pl.BlockSpec((tile, SEG_W), lambda b, t: (b * tiles_per_seq + t, seg))

    def pseg_spec(seg):
        return pl.BlockSpec((PREFIX, SEG_W), lambda b, t: (0, seg))

    return pl.pallas_call(
        _hgrn_kernel,
        grid=(batch, tiles_per_seq),
        in_specs=[
            pseg_spec(SEG_HQ), pseg_spec(SEG_HF), pseg_spec(SEG_HI), pseg_spec(0),
            seg_spec(SEG_HQ), seg_spec(SEG_HF), seg_spec(SEG_HI), seg_spec(SEG_HZ), seg_spec(0),
            pl.BlockSpec((1, HEAD_W), lambda b, t: (0, 0)),
        ],
        out_specs=seg_spec(0),
        out_shape=jax.ShapeDtypeStruct((batch * seq, SEG_W), BF16),
        scratch_shapes=[pltpu.VMEM((N_HEADS, HEAD_W, HEAD_W), F32)],
        compiler_params=pltpu.CompilerParams(
            dimension_semantics=("parallel", "arbitrary"), vmem_limit_bytes=VMEM_LIMIT),
        name="hgrn2",
    )(pproj, pproj, pproj, pglog, proj, proj, proj, proj, glog, gain)


def _attn_kernel(lam_ref, pk_ref, pvt_ref, q_ref, k_ref, vt_ref, z_ref, gaint_ref, o_ref,
                 acc_ref, qqt_ref, s_ref, m_ref, l_ref):
    bq, bk = ATT_BQ, ATT_BK
    nsub = bq // HEAD_W
    n_q = q_ref.shape[0] // bq
    lam = lam_ref[0, 0]
    steps = [(i, j) for i in range(n_q) for j in range(i + 1)]

    drow = lax.broadcasted_iota(jnp.int32, (HEAD_W, HEAD_W), 0)
    first = drow < DA_D
    pvalid = lax.broadcasted_iota(jnp.int32, (PREFIX, ATT_STRIP), 0) >= N_INERT

    def causal(c):
        kr = key_rows(c)
        krow = lax.broadcasted_iota(jnp.int32, (kr, ATT_STRIP), 0)
        query = lax.broadcasted_iota(jnp.int32, (kr, ATT_STRIP), 1) + (c * ATT_STRIP) % bq
        return krow <= query

    def key_rows(c):
        return min(bk, (c * ATT_STRIP) % bq + ATT_STRIP)

    def build_q(i):
        qts = [q_ref[i * bq + t * HEAD_W:i * bq + (t + 1) * HEAD_W, :].astype(F32).T
               for t in range(nsub)]
        qqt_ref[i % 2] = jnp.concatenate(
            [jnp.where(first, qt, 0.0) for qt in qts] + [jnp.where(first, 0.0, qt) for qt in qts],
            axis=1).astype(BF16)

    def issue_scores(t, c):
        i, j = steps[t]
        cols = slice(c * ATT_STRIP, (c + 1) * ATT_STRIP)
        qs = qqt_ref[i % 2, :, cols]
        kr = key_rows(c) if j == i else bk
        if j == 0:
            keys = jnp.concatenate([pk_ref[...], k_ref[0:kr, :]], axis=0)
            s_ref[t % 2, 0:PREFIX + kr, cols] = jnp.dot(keys, qs, preferred_element_type=F32)
        else:
            s_ref[t % 2, 0:kr, cols] = jnp.dot(k_ref[j * bk:j * bk + kr, :], qs,
                                               preferred_element_type=F32)

    def update(t, c):
        i, j = steps[t]
        par = i % 2
        cols = slice(c * ATT_STRIP, (c + 1) * ATT_STRIP)
        kr = key_rows(c) if j == i else bk
        if j == 0:
            s = s_ref[t % 2, 0:PREFIX + kr, cols]
            body = s[PREFIX:, :]
            if i == 0:
                body = jnp.where(causal(c), body, MASK_VALUE)
            s = jnp.concatenate([jnp.where(pvalid, s[0:PREFIX, :], MASK_VALUE), body], axis=0)
            vtb = jnp.concatenate([pvt_ref[0], vt_ref[0, :, 0:kr]], axis=1)
            m_new = jnp.max(s, axis=0, keepdims=True)
            p = jnp.exp2(s - m_new)
            m_ref[par, :, cols] = m_new
            l_ref[par, :, cols] = jnp.sum(p, axis=0, keepdims=True)
            acc_ref[par, :, cols] = jnp.dot(vtb, p.astype(BF16), preferred_element_type=F32)
        else:
            s = s_ref[t % 2, 0:kr, cols]
            if j == i:
                s = jnp.where(causal(c), s, MASK_VALUE)
            m = m_ref[par, :, cols]
            m_new = jnp.maximum(m, jnp.max(s, axis=0, keepdims=True))
            alpha = jnp.exp2(m - m_new)
            p = jnp.exp2(s - m_new)
            m_ref[par, :, cols] = m_new
            l_ref[par, :, cols] = alpha * l_ref[par, :, cols] + jnp.sum(p, axis=0, keepdims=True)
            acc_ref[par, :, cols] = alpha * acc_ref[par, :, cols] + jnp.dot(
                vt_ref[j, :, 0:kr], p.astype(BF16), preferred_element_type=F32)

    def epilogue(i):
        par = i % 2
        out = acc_ref[par] * (1.0 / l_ref[par])
        ot = out[:, :bq] - lam * out[:, bq:]
        ms = jnp.mean(ot * ot, axis=0, keepdims=True)
        ot = ot * (lax.rsqrt(ms + EPS) * (1.0 - LAM_INIT))
        for t in range(nsub):
            cols = slice(t * HEAD_W, (t + 1) * HEAD_W)
            o = (ot[:, cols] * gaint_ref[...]).T
            rows = slice(i * bq + t * HEAD_W, i * bq + (t + 1) * HEAD_W)
            z = z_ref[rows, :].astype(F32)
            o_ref[rows, :] = (o * (z * _sigmoid(z))).astype(BF16)

    n_strips = 2 * bq // ATT_STRIP
    build_q(0)
    for c in range(n_strips):
        issue_scores(0, c)
    for t, (i, j) in enumerate(steps):
        has_next = t + 1 < len(steps)
        if has_next and steps[t + 1][1] == 0:
            build_q(steps[t + 1][0])
        for c in range(n_strips):
            if has_next:
                issue_scores(t + 1, c)
            update(t, c)
        if j == i:
            epilogue(i)


def _attn(proj, vt, pproj, pvt, lam, gain, *, batch, seq):
    def seg_spec(seg):
        return pl.BlockSpec((seq, HEAD_W), lambda b, h: (b, seg * N_HEADS + h))

    def pseg_spec(seg):
        return pl.BlockSpec((PREFIX, HEAD_W), lambda b, h: (0, seg * N_HEADS + h))

    return pl.pallas_call(
        _attn_kernel,
        grid=(batch, N_HEADS),
        in_specs=[
            pl.BlockSpec(memory_space=pltpu.SMEM),
            pseg_spec(SEG_AK),
            pl.BlockSpec((1, HEAD_W, PREFIX), lambda b, h: (0, h, 0)),
            seg_spec(SEG_AQ), seg_spec(SEG_AK),
            pl.BlockSpec((seq // ATT_BK, HEAD_W, ATT_BK), lambda b, h: (b, h, 0)),
            seg_spec(SEG_AZ),
            pl.BlockSpec((HEAD_W, HEAD_W), lambda b, h: (0, 0)),
        ],
        out_specs=pl.BlockSpec((seq, HEAD_W), lambda b, h: (b, h)),
        out_shape=jax.ShapeDtypeStruct((batch * seq, SEG_W), BF16),
        scratch_shapes=[pltpu.VMEM((2, HEAD_W, 2 * ATT_BQ), F32),
                        pltpu.VMEM((2, HEAD_W, 2 * ATT_BQ), BF16),
                        pltpu.VMEM((2, ATT_BK + PREFIX, 2 * ATT_BQ), F32),
                        pltpu.VMEM((2, 1, 2 * ATT_BQ), F32),
                        pltpu.VMEM((2, 1, 2 * ATT_BQ), F32)],
        compiler_params=pltpu.CompilerParams(
            dimension_semantics=("parallel", "parallel"), vmem_limit_bytes=VMEM_LIMIT),
        name="diff_attn",
    )(lam, pproj, pvt, proj, proj, vt, proj, gain)


def _out_kernel(x_ref, oa_ref, ob_ref, ga_ref, gb_ref, wa_ref, wb_ref, wo_ref, fg_ref, out_ref):
    ya = jnp.dot(oa_ref[...], wa_ref[...], preferred_element_type=F32)
    yb = jnp.dot(ob_ref[...], wb_ref[...], preferred_element_type=F32)
    m = _sigmoid(ga_ref[...].astype(F32)) * ya + _sigmoid(gb_ref[...].astype(F32)) * yb
    h = x_ref[...] + jnp.dot(m.astype(BF16), wo_ref[...], preferred_element_type=F32)
    ms = jnp.mean(h * h, axis=-1, keepdims=True)
    out_ref[...] = h * lax.rsqrt(ms + EPS) * fg_ref[...]


def _out_stage(x2d, oa, ob, proj, wa, wb, wo, fg, *, tm):
    n_rows = x2d.shape[0]
    tile = lambda col: pl.BlockSpec((tm, SEG_W), lambda i: (i, col))
    wspec = pl.BlockSpec((SEG_W, D_MODEL), lambda i: (0, 0))
    return pl.pallas_call(
        _out_kernel,
        grid=(n_rows // tm,),
        in_specs=[tile(0), tile(0), tile(0), tile(SEG_GA), tile(SEG_GB), wspec, wspec, wspec,
                  pl.BlockSpec((1, D_MODEL), lambda i: (0, 0))],
        out_specs=tile(0),
        out_shape=jax.ShapeDtypeStruct((n_rows, D_MODEL), F32),
        compiler_params=pltpu.CompilerParams(
            dimension_semantics=("parallel",), vmem_limit_bytes=VMEM_LIMIT),
        name="out_stage",
    )(x2d, oa, ob, proj, proj, wa, wb, wo, fg)


def _rope_tables(pos):
    lane = jnp.arange(HEAD_W) % DA_D
    inv = ROPE_THETA ** (-(lane % ROPE_HALF).astype(F32) * 2.0 / ROPE_DIM)
    ang = pos.astype(F32)[:, None] * inv[None, :]
    cos, sin = jnp.cos(ang), jnp.sin(ang)
    lo = (lane < ROPE_HALF)[None, :]
    hi = jnp.logical_and(lane >= ROPE_HALF, lane < ROPE_DIM)[None, :]
    cos_t = jnp.where(jnp.logical_or(lo, hi), cos, 1.0)
    sa_t = jnp.where(hi, sin, 0.0)
    sb_t = jnp.where(lo, -sin, 0.0)
    return cos_t, sa_t, sb_t


def kernel(x, meta_tokens, norm_g, w_in, hg_lb_logits, hg_norm_g, da_lambda, da_norm_g,
           w_branch_a, w_branch_b, w_out, final_g):
    batch, seq, d = x.shape
    assert d == D_MODEL and seq % ATT_BQ == 0 and seq % 1024 == 0
    assert norm_g.shape[0] == 1, "single-layer block"

    x2d = x.reshape(batch * seq, d)
    prefix = jnp.concatenate([jnp.zeros((N_INERT, d), x.dtype), meta_tokens.astype(x.dtype)], axis=0)

    lb = jnp.cumsum(jax.nn.softmax(hg_lb_logits.astype(F32), axis=0), axis=0)[0][None, :]
    lp = da_lambda[0].astype(F32)
    lam = (jnp.exp(jnp.sum(lp[0] * lp[1])) - jnp.exp(jnp.sum(lp[2] * lp[3])) + LAM_INIT).reshape(1, 1)

    w_bf = w_in[0].astype(BF16).reshape(D_MODEL, N_SEG + 1, SEG_W).transpose(1, 0, 2)
    w_vt = w_in[0][:, REF_SEG_AV * SEG_W:(REF_SEG_AV + 1) * SEG_W].T.astype(BF16)
    ng = norm_g[0][None, :].astype(F32)
    tabs_main = _rope_tables(jnp.arange(seq) + N_META)
    tabs_pre = _rope_tables(jnp.maximum(jnp.arange(PREFIX) - N_INERT, 0))

    proj, glog, vt = _inproj(x2d, ng, w_bf, w_vt, lb, *tabs_main, tm=1024, rows_per_seq=seq,
                             n_inert=0)
    pproj, pglog, pvt = _inproj(prefix, ng, w_bf, w_vt, lb, *tabs_pre, tm=PREFIX,
                                rows_per_seq=PREFIX, n_inert=N_INERT)

    oa = _hgrn(proj, glog, pproj, pglog, hg_norm_g[0][None, :].astype(F32), batch=batch, seq=seq,
               tile=512)
    da_gain_rows = jnp.broadcast_to(da_norm_g[0].astype(F32)[:, None], (HEAD_W, HEAD_W))
    ob = _attn(proj, vt, pproj, pvt, lam, da_gain_rows, batch=batch, seq=seq)

    out = _out_stage(x2d, oa, ob, proj, w_branch_a[0].astype(BF16), w_branch_b[0].astype(BF16),
                     w_out[0].astype(BF16), final_g[None, :].astype(F32), tm=512)
    return out.reshape(batch, seq, d)
```

```python
import functools

import jax
import jax.numpy as jnp
import numpy as np
from jax import lax
from jax.experimental import pallas as pl
from jax.experimental.pallas import tpu as pltpu

F32 = jnp.float32
BF16 = jnp.bfloat16

D_MODEL = 1024
N_HEADS = 8
HEAD_W = 128
DA_D = 64
N_META = 16
PREFIX = 128
N_INERT = PREFIX - N_META
ROPE_DIM = 16
ROPE_HALF = ROPE_DIM // 2
ROPE_THETA = 500000.0
EPS = 1e-6
LAM_INIT = 0.2
MASK_VALUE = -1e30
Q_SCALE = DA_D ** -0.5 * float(np.log2(np.e))

SEG_W = 1024
N_SEG = 9
SEG_HQ, SEG_HF, SEG_HI, SEG_HZ, SEG_AQ, SEG_AK, SEG_AZ, SEG_GA, SEG_GB = range(N_SEG)
REF_SEG_AV = 6

CHUNK = 128
SUB = 16
N_SUB = CHUNK // SUB
ATT_BQ = 512
ATT_BK = 512
ATT_STRIP = 256

VMEM_LIMIT = 56 * 1024 * 1024
INPROJ_VMEM_LIMIT = 58 * 1024 * 1024


def _sigmoid(x):
    return 1.0 / (1.0 + jnp.exp(-x))


def _inproj_kernel(n_inert, x_ref, ng_ref, *refs):
    w_refs = refs[:N_SEG]
    wvt_ref, lb_ref, cos_ref, sa_ref, sb_ref, out_ref, glog_ref, vt_ref, u_ref = refs[N_SEG:]
    x = x_ref[...]
    ms = jnp.mean(x * x, axis=-1, keepdims=True)
    u_ref[...] = (x * lax.rsqrt(ms + EPS) * ng_ref[...]).astype(BF16)

    for seg in range(N_SEG):
        acc = jnp.dot(u_ref[...], w_refs[seg][...], preferred_element_type=F32)
        cols = slice(seg * SEG_W, (seg + 1) * SEG_W)
        if seg == SEG_HF:
            lb = lb_ref[...]
            f = lb + (1.0 - lb) * _sigmoid(acc)
            if n_inert:
                ridx = lax.broadcasted_iota(jnp.int32, f.shape, 0)
                f = jnp.where(ridx >= n_inert, f, 1.0)
            glog_ref[...] = jnp.log2(f)
            out_ref[:, cols] = (1.0 - f).astype(BF16)
        elif seg in (SEG_AQ, SEG_AK):
            scale = Q_SCALE if seg == SEG_AQ else 1.0
            cos = cos_ref[...] * scale
            sa = sa_ref[...] * scale
            sb = sb_ref[...] * scale
            for c in range(SEG_W // HEAD_W):
                xs = acc[:, c * HEAD_W:(c + 1) * HEAD_W]
                y = (xs * cos + pltpu.roll(xs, ROPE_HALF, 1) * sa
                     + pltpu.roll(xs, HEAD_W - ROPE_HALF, 1) * sb)
                out_ref[:, seg * SEG_W + c * HEAD_W:seg * SEG_W + (c + 1) * HEAD_W] = y.astype(BF16)
        else:
            out_ref[:, cols] = acc.astype(BF16)

    acc_t = lax.dot_general(wvt_ref[...], u_ref[...], (((1,), (1,)), ((), ())),
                            preferred_element_type=F32)
    vt_ref[0] = acc_t.astype(BF16)


def _inproj(x2d, norm_g, w_segs, w_vt, lb, cos_t, sa_t, sb_t, *, tm, rows_per_seq, n_inert):
    n_rows = x2d.shape[0]
    tiles_per_seq = rows_per_seq // tm
    row_spec = pl.BlockSpec((1, D_MODEL), lambda i: (0, 0))
    tab_spec = pl.BlockSpec((tm, HEAD_W), lambda i: (i % tiles_per_seq, 0))
    w_spec = pl.BlockSpec((D_MODEL, SEG_W), lambda i: (0, 0), pipeline_mode=pl.Buffered(1))
    return pl.pallas_call(
        functools.partial(_inproj_kernel, n_inert),
        grid=(n_rows // tm,),
        in_specs=[pl.BlockSpec((tm, D_MODEL), lambda i: (i, 0)), row_spec]
        + [w_spec] * (N_SEG + 1) + [row_spec, tab_spec, tab_spec, tab_spec],
        out_specs=[
            pl.BlockSpec((tm, N_SEG * SEG_W), lambda i: (i, 0)),
            pl.BlockSpec((tm, SEG_W), lambda i: (i, 0)),
            pl.BlockSpec((1, D_MODEL, tm), lambda i: (i, 0, 0)),
        ],
        out_shape=[
            jax.ShapeDtypeStruct((n_rows, N_SEG * SEG_W), BF16),
            jax.ShapeDtypeStruct((n_rows, SEG_W), F32),
            jax.ShapeDtypeStruct((n_rows // tm, D_MODEL, tm), BF16),
        ],
        scratch_shapes=[pltpu.VMEM((tm, D_MODEL), BF16)],
        compiler_params=pltpu.CompilerParams(
            dimension_semantics=("parallel",), vmem_limit_bytes=INPROJ_VMEM_LIMIT),
        name="inproj",
    )(x2d, norm_g, *w_segs, w_vt, lb, cos_t, sa_t, sb_t)


def _bcast_rows(rows, n):
    return jnp.concatenate([jnp.broadcast_to(r, (n, r.shape[1])) for r in rows], axis=0)


def _hgrn_chunk(q, k, v, g, sts, tril):
    n_heads = len(sts)
    width = q.shape[1]
    head = lambda x, h: x[:, h * HEAD_W:(h + 1) * HEAD_W]
    nt = (((1,), (1,)), ((), ()))

    g_hi = g.astype(BF16)
    g_lo = (g - g_hi.astype(F32)).astype(BF16)
    bs = jnp.dot(tril.astype(BF16), jnp.concatenate([g_hi, g_lo], axis=1),
                 preferred_element_type=F32)
    b = bs[:, :width] + bs[:, width:]

    ends = [b[SUB * s + SUB - 1:SUB * s + SUB, :] for s in range(N_SUB)]
    starts = [jnp.zeros_like(ends[0])] + ends[:-1]
    end_b = _bcast_rows(ends, SUB)
    start_b = _bcast_rows(starts, SUB)
    b_last = ends[-1]

    qd = q * jnp.exp2(b - start_b)
    khat = k * jnp.exp2(end_b - b)
    qd_bf = qd.astype(BF16)
    khat_bf = khat.astype(BF16)
    q_s = qd_bf * _bcast_rows([jnp.exp2(s) for s in starts], SUB).astype(BF16)
    k_s = khat_bf * _bcast_rows([jnp.exp2(b_last - e) for e in ends], SUB).astype(BF16)
    decay = jnp.exp2(b_last)

    o_inter = [lax.dot_general(head(q_s, h), sts[h].astype(BF16), nt, preferred_element_type=F32)
               for h in range(n_heads)]
    st_new = [sts[h] * head(decay, h)
              + lax.dot_general(head(v, h), head(k_s, h), (((0,), (0,)), ((), ())),
                                preferred_element_type=F32)
              for h in range(n_heads)]

    a_rows = [[] for _ in range(n_heads)]
    for i in range(N_SUB):
        parts = []
        for jj in range(i + 1):
            kj = khat_bf[SUB * jj:SUB * jj + SUB, :]
            if jj == i:
                w = jnp.exp2(starts[i] - ends[i])
            elif jj == i - 1:
                w = None
            else:
                w = jnp.exp2(starts[i] - ends[jj])
            parts.append(kj if w is None else kj * jnp.broadcast_to(w, kj.shape).astype(BF16))
        if i + 1 < N_SUB:
            parts.append(jnp.zeros((CHUNK - SUB * (i + 1), width), BF16))
        km = jnp.concatenate(parts, axis=0)
        qi = qd_bf[SUB * i:SUB * i + SUB, :]
        for h in range(n_heads):
            a_rows[h].append(lax.dot_general(head(qi, h), head(km, h), nt,
                                             preferred_element_type=F32))
    outs = []
    for h in range(n_heads):
        a = jnp.where(tril, jnp.concatenate(a_rows[h], axis=0), 0.0).astype(BF16)
        outs.append(jnp.dot(a, head(v, h), preferred_element_type=F32) + o_inter[h])
    return outs, st_new


def _hgrn_kernel(pq_ref, pk_ref, pv_ref, pg_ref, q_ref, k_ref, v_ref, z_ref, g_ref, gain_ref, o_ref,
                 st_ref):
    n_chunks = q_ref.shape[0] // CHUNK
    tril = (lax.broadcasted_iota(jnp.int32, (CHUNK, CHUNK), 0)
            >= lax.broadcasted_iota(jnp.int32, (CHUNK, CHUNK), 1))
    gain = gain_ref[...]

    @pl.when(pl.program_id(1) == 0)
    def _():
        _, sts = _hgrn_chunk(pq_ref[...].astype(F32), pk_ref[...].astype(F32), pv_ref[...],
                             pg_ref[...], [jnp.zeros((HEAD_W, HEAD_W), F32)] * N_HEADS, tril)
        for h in range(N_HEADS):
            st_ref[h] = sts[h]

    sts = [st_ref[h] for h in range(N_HEADS)]
    for c in range(n_chunks):
        rows = slice(c * CHUNK, (c + 1) * CHUNK)
        outs, sts = _hgrn_chunk(q_ref[rows, :].astype(F32), k_ref[rows, :].astype(F32),
                                v_ref[rows, :], g_ref[rows, :], sts, tril)
        for h in range(N_HEADS):
            o = outs[h]
            cols = slice(h * HEAD_W, (h + 1) * HEAD_W)
            ms = jnp.mean(o * o, axis=-1, keepdims=True)
            on = o * lax.rsqrt(ms + EPS) * gain
            z = z_ref[rows, cols].astype(F32)
            o_ref[rows, cols] = (on * (z * _sigmoid(z))).astype(BF16)
    for h in range(N_HEADS):
        st_ref[h] = sts[h]


def _hgrn(proj, glog, pproj, pglog, gain, *, batch, seq, tile):
    tiles_per_seq = seq // tile

    def seg_spec(seg):
        return pl.BlockSpec((tile, SEG_W), lambda b, t: (b * tiles_per_seq + t, seg))

    def pseg_spec(seg):
        return pl.BlockSpec((PREFIX, SEG_W), lambda b, t: (0, seg))

    return pl.pallas_call(
        _hgrn_kernel,
        grid=(batch, tiles_per_seq),
        in_specs=[
            pseg_spec(SEG_HQ), pseg_spec(SEG_HF), pseg_spec(SEG_HI), pseg_spec(0),
            seg_spec(SEG_HQ), seg_spec(SEG_HF), seg_spec(SEG_HI), seg_spec(SEG_HZ), seg_spec(0),
            pl.BlockSpec((1, HEAD_W), lambda b, t: (0, 0)),
        ],
        out_specs=seg_spec(0),
        out_shape=jax.ShapeDtypeStruct((batch * seq, SEG_W), BF16),
        scratch_shapes=[pltpu.VMEM((N_HEADS, HEAD_W, HEAD_W), F32)],
        compiler_params=pltpu.CompilerParams(
            dimension_semantics=("parallel", "arbitrary"), vmem_limit_bytes=VMEM_LIMIT),
        name="hgrn2",
    )(pproj, pproj, pproj, pglog, proj, proj, proj, proj, glog, gain)


def _attn_kernel(lam_ref, pk_ref, pvt_ref, q_ref, k_ref, vt_ref, z_ref, gaint_ref, o_ref,
                 acc_ref, qqt_ref, s_ref, m_ref, l_ref):
    bq, bk = ATT_BQ, ATT_BK
    nsub = bq // HEAD_W
    n_q = q_ref.shape[0] // bq
    lam = lam_ref[0, 0]
    steps = [(i, j) for i in range(n_q) for j in range(i + 1)]

    drow = lax.broadcasted_iota(jnp.int32, (HEAD_W, HEAD_W), 0)
    first = drow < DA_D
    pvalid = lax.broadcasted_iota(jnp.int32, (PREFIX, ATT_STRIP), 0) >= N_INERT

    def causal(c):
        kr = key_rows(c)
        krow = lax.broadcasted_iota(jnp.int32, (kr, ATT_STRIP), 0)
        query = lax.broadcasted_iota(jnp.int32, (kr, ATT_STRIP), 1) + (c * ATT_STRIP) % bq
        return krow <= query

    def key_rows(c):
        return min(bk, (c * ATT_STRIP) % bq + ATT_STRIP)

    def build_q(i):
        qts = [q_ref[i * bq + t * HEAD_W:i * bq + (t + 1) * HEAD_W, :].astype(F32).T
               for t in range(nsub)]
        qqt_ref[i % 2] = jnp.concatenate(
            [jnp.where(first, qt, 0.0) for qt in qts] + [jnp.where(first, 0.0, qt) for qt in qts],
            axis=1).astype(BF16)

    def issue_scores(t, c):
        i, j = steps[t]
        cols = slice(c * ATT_STRIP, (c + 1) * ATT_STRIP)
        qs = qqt_ref[i % 2, :, cols]
        kr = key_rows(c) if j == i else bk
        if j == 0:
            keys = jnp.concatenate([pk_ref[...], k_ref[0:kr, :]], axis=0)
            s_ref[t % 2, 0:PREFIX + kr, cols] = jnp.dot(keys, qs, preferred_element_type=F32)
        else:
            s_ref[t % 2, 0:kr, cols] = jnp.dot(k_ref[j * bk:j * bk + kr, :], qs,
                                               preferred_element_type=F32)

    def update(t, c):
        i, j = steps[t]
        par = i % 2
        cols = slice(c * ATT_STRIP, (c + 1) * ATT_STRIP)
        kr = key_rows(c) if j == i else bk
        if j == 0:
            s = s_ref[t % 2, 0:PREFIX + kr, cols]
            body = s[PREFIX:, :]
            if i == 0:
                body = jnp.where(causal(c), body, MASK_VALUE)
            s = jnp.concatenate([jnp.where(pvalid, s[0:PREFIX, :], MASK_VALUE), body], axis=0)
            vtb = jnp.concatenate([pvt_ref[0], vt_ref[0, :, 0:kr]], axis=1)
            m_new = jnp.max(s, axis=0, keepdims=True)
            p = jnp.exp2(s - m_new)
            m_ref[par, :, cols] = m_new
            l_ref[par, :, cols] = jnp.sum(p, axis=0, keepdims=True)
            acc_ref[par, :, cols] = jnp.dot(vtb, p.astype(BF16), preferred_element_type=F32)
        else:
            s = s_ref[t % 2, 0:kr, cols]
            if j == i:
                s = jnp.where(causal(c), s, MASK_VALUE)
            m = m_ref[par, :, cols]
            m_new = jnp.maximum(m, jnp.max(s, axis=0, keepdims=True))
            alpha = jnp.exp2(m - m_new)
            p = jnp.exp2(s - m_new)
            m_ref[par, :, cols] = m_new
            l_ref[par, :, cols] = alpha * l_ref[par, :, cols] + jnp.sum(p, axis=0, keepdims=True)
            acc_ref[par, :, cols] = alpha * acc_ref[par, :, cols] + jnp.dot(
                vt_ref[j, :, 0:kr], p.astype(BF16), preferred_element_type=F32)

    def epilogue(i):
        par = i % 2
        out = acc_ref[par] * (1.0 / l_ref[par])
        ot = out[:, :bq] - lam * out[:, bq:]
        ms = jnp.mean(ot * ot, axis=0, keepdims=True)
        ot = ot * (lax.rsqrt(ms + EPS) * (1.0 - LAM_INIT))
        for t in range(nsub):
            cols = slice(t * HEAD_W, (t + 1) * HEAD_W)
            o = (ot[:, cols] * gaint_ref[...]).T
            rows = slice(i * bq + t * HEAD_W, i * bq + (t + 1) * HEAD_W)
            z = z_ref[rows, :].astype(F32)
            o_ref[rows, :] = (o * (z * _sigmoid(z))).astype(BF16)

    n_strips = 2 * bq // ATT_STRIP
    build_q(0)
    for c in range(n_strips):
        issue_scores(0, c)
    for t, (i, j) in enumerate(steps):
        has_next = t + 1 < len(steps)
        if has_next and steps[t + 1][1] == 0:
            build_q(steps[t + 1][0])
        for c in range(n_strips):
            if has_next:
                issue_scores(t + 1, c)
            update(t, c)
        if j == i:
            epilogue(i)


def _attn(proj, vt, pproj, pvt, lam, gain, *, batch, seq):
    def seg_spec(seg):
        return pl.BlockSpec((seq, HEAD_W), lambda b, h: (b, seg * N_HEADS + h))

    def pseg_spec(seg):
        return pl.BlockSpec((PREFIX, HEAD_W), lambda b, h: (0, seg * N_HEADS + h))

    return pl.pallas_call(
        _attn_kernel,
        grid=(batch, N_HEADS),
        in_specs=[
            pl.BlockSpec(memory_space=pltpu.SMEM),
            pseg_spec(SEG_AK),
            pl.BlockSpec((1, HEAD_W, PREFIX), lambda b, h: (0, h, 0)),
            seg_spec(SEG_AQ), seg_spec(SEG_AK),
            pl.BlockSpec((seq // ATT_BK, HEAD_W, ATT_BK), lambda b, h: (b, h, 0)),
            seg_spec(SEG_AZ),
            pl.BlockSpec((HEAD_W, HEAD_W), lambda b, h: (0, 0)),
        ],
        out_specs=pl.BlockSpec((seq, HEAD_W), lambda b, h: (b, h)),
        out_shape=jax.ShapeDtypeStruct((batch * seq, SEG_W), BF16),
        scratch_shapes=[pltpu.VMEM((2, HEAD_W, 2 * ATT_BQ), F32),
                        pltpu.VMEM((2, HEAD_W, 2 * ATT_BQ), BF16),
                        pltpu.VMEM((2, ATT_BK + PREFIX, 2 * ATT_BQ), F32),
                        pltpu.VMEM((2, 1, 2 * ATT_BQ), F32),
                        pltpu.VMEM((2, 1, 2 * ATT_BQ), F32)],
        compiler_params=pltpu.CompilerParams(
            dimension_semantics=("parallel", "parallel"), vmem_limit_bytes=VMEM_LIMIT),
        name="diff_attn",
    )(lam, pproj, pvt, proj, proj, vt, proj, gain)


def _out_kernel(x_ref, oa_ref, ob_ref, ga_ref, gb_ref, wa_ref, wb_ref, wo_ref, fg_ref, out_ref):
    ya = jnp.dot(oa_ref[...], wa_ref[...], preferred_element_type=F32)
    yb = jnp.dot(ob_ref[...], wb_ref[...], preferred_element_type=F32)
    m = _sigmoid(ga_ref[...].astype(F32)) * ya + _sigmoid(gb_ref[...].astype(F32)) * yb
    h = x_ref[...] + jnp.dot(m.astype(BF16), wo_ref[...], preferred_element_type=F32)
    ms = jnp.mean(h * h, axis=-1, keepdims=True)
    out_ref[...] = h * lax.rsqrt(ms + EPS) * fg_ref[...]


def _out_stage(x2d, oa, ob, proj, wa, wb, wo, fg, *, tm):
    n_rows = x2d.shape[0]
    tile = lambda col: pl.BlockSpec((tm, SEG_W), lambda i: (i, col))
    wspec = pl.BlockSpec((SEG_W, D_MODEL), lambda i: (0, 0))
    return pl.pallas_call(
        _out_kernel,
        grid=(n_rows // tm,),
        in_specs=[tile(0), tile(0), tile(0), tile(SEG_GA), tile(SEG_GB), wspec, wspec, wspec,
                  pl.BlockSpec((1, D_MODEL), lambda i: (0, 0))],
        out_specs=tile(0),
        out_shape=jax.ShapeDtypeStruct((n_rows, D_MODEL), F32),
        compiler_params=pltpu.CompilerParams(
            dimension_semantics=("parallel",), vmem_limit_bytes=VMEM_LIMIT),
        name="out_stage",
    )(x2d, oa, ob, proj, proj, wa, wb, wo, fg)


def _rope_tables(pos):
    lane = jnp.arange(HEAD_W) % DA_D
    inv = ROPE_THETA ** (-(lane % ROPE_HALF).astype(F32) * 2.0 / ROPE_DIM)
    ang = pos.astype(F32)[:, None] * inv[None, :]
    cos, sin = jnp.cos(ang), jnp.sin(ang)
    lo = (lane < ROPE_HALF)[None, :]
    hi = jnp.logical_and(lane >= ROPE_HALF, lane < ROPE_DIM)[None, :]
    cos_t = jnp.where(jnp.logical_or(lo, hi), cos, 1.0)
    sa_t = jnp.where(hi, sin, 0.0)
    sb_t = jnp.where(lo, -sin, 0.0)
    return cos_t, sa_t, sb_t


def kernel(x, meta_tokens, norm_g, w_in, hg_lb_logits, hg_norm_g, da_lambda, da_norm_g,
           w_branch_a, w_branch_b, w_out, final_g):
    batch, seq, d = x.shape
    assert d == D_MODEL and seq % ATT_BQ == 0 and ATT_BQ == ATT_BK
    assert norm_g.shape[0] == 1, "single-layer block"

    x2d = x.reshape(batch * seq, d)
    prefix = jnp.concatenate([jnp.zeros((N_INERT, d), x.dtype), meta_tokens.astype(x.dtype)], axis=0)

    lb = jnp.cumsum(jax.nn.softmax(hg_lb_logits.astype(F32), axis=0), axis=0)[0][None, :]
    lp = da_lambda[0].astype(F32)
    lam = (jnp.exp(jnp.sum(lp[0] * lp[1])) - jnp.exp(jnp.sum(lp[2] * lp[3])) + LAM_INIT).reshape(1, 1)

    ref_segs = [r for r in range(N_SEG + 1) if r != REF_SEG_AV]
    w_bf = [w_in[0][:, r * SEG_W:(r + 1) * SEG_W].astype(BF16) for r in ref_segs]
    w_vt = w_in[0][:, REF_SEG_AV * SEG_W:(REF_SEG_AV + 1) * SEG_W].T.astype(BF16)
    ng = norm_g[0][None, :].astype(F32)
    tabs_main = _rope_tables(jnp.arange(seq) + N_META)
    tabs_pre = _rope_tables(jnp.maximum(jnp.arange(PREFIX) - N_INERT, 0))

    proj, glog, vt = _inproj(x2d, ng, w_bf, w_vt, lb, *tabs_main, tm=ATT_BK, rows_per_seq=seq,
                             n_inert=0)
    pproj, pglog, pvt = _inproj(prefix, ng, w_bf, w_vt, lb, *tabs_pre, tm=PREFIX,
                                rows_per_seq=PREFIX, n_inert=N_INERT)

    oa = _hgrn(proj, glog, pproj, pglog, hg_norm_g[0][None, :].astype(F32), batch=batch, seq=seq,
               tile=512)
    da_gain_rows = jnp.broadcast_to(da_norm_g[0].astype(F32)[:, None], (HEAD_W, HEAD_W))
    ob = _attn(proj, vt, pproj, pvt, lam, da_gain_rows, batch=batch, seq=seq)

    out = _out_stage(x2d, oa, ob, proj, w_branch_a[0].astype(BF16), w_branch_b[0].astype(BF16),
                     w_out[0].astype(BF16), final_g[None, :].astype(F32), tm=512)
    return out.reshape(batch, seq, d)
```

```python
import functools

import jax
import jax.numpy as jnp
import numpy as np
from jax import lax
from jax.experimental import pallas as pl
from jax.experimental.pallas import tpu as pltpu

F32 = jnp.float32
BF16 = jnp.bfloat16

D_MODEL = 1024
N_HEADS = 8
HEAD_W = 128
DA_D = 64
N_META = 16
PREFIX = 128
N_INERT = PREFIX - N_META
ROPE_DIM = 16
ROPE_HALF = ROPE_DIM // 2
ROPE_THETA = 500000.0
EPS = 1e-6
LAM_INIT = 0.2
MASK_VALUE = -1e30
Q_SCALE = DA_D ** -0.5 * float(np.log2(np.e))

SEG_W = 1024
N_SEG = 9
SEG_HQ, SEG_HF, SEG_HI, SEG_HZ, SEG_AQ, SEG_AK, SEG_AZ, SEG_GA, SEG_GB = range(N_SEG)
REF_SEG_AV = 6

CHUNK = 128
SUB = 16
N_SUB = CHUNK // SUB
ATT_BQ = 512
ATT_BK = 512
ATT_STRIP = 256

VMEM_LIMIT = 56 * 1024 * 1024
INPROJ_VMEM_LIMIT = 58 * 1024 * 1024


def _sigmoid(x):
    return 1.0 / (1.0 + jnp.exp(-x))


def _inproj_kernel(n_inert, x_ref, ng_ref, w_ref, lb_ref, cos_ref, sa_ref, sb_ref,
                   out_ref, glog_ref, vt_ref, u_ref):
    x = x_ref[...]
    ms = jnp.mean(x * x, axis=-1, keepdims=True)
    u_ref[...] = (x * lax.rsqrt(ms + EPS) * ng_ref[...]).astype(BF16)

    for seg in range(N_SEG):
        ref_seg = seg + (seg >= REF_SEG_AV)
        acc = jnp.dot(u_ref[...], w_ref[:, ref_seg * SEG_W:(ref_seg + 1) * SEG_W],
                      preferred_element_type=F32)
        cols = slice(seg * SEG_W, (seg + 1) * SEG_W)
        if seg == SEG_HF:
            lb = lb_ref[...]
            f = lb + (1.0 - lb) * _sigmoid(acc)
            if n_inert:
                ridx = lax.broadcasted_iota(jnp.int32, f.shape, 0)
                f = jnp.where(ridx >= n_inert, f, 1.0)
            glog_ref[...] = jnp.log2(f)
            out_ref[:, cols] = (1.0 - f).astype(BF16)
        elif seg in (SEG_AQ, SEG_AK):
            scale = Q_SCALE if seg == SEG_AQ else 1.0
            cos = cos_ref[...] * scale
            sa = sa_ref[...] * scale
            sb = sb_ref[...] * scale
            for c in range(SEG_W // HEAD_W):
                xs = acc[:, c * HEAD_W:(c + 1) * HEAD_W]
                y = (xs * cos + pltpu.roll(xs, ROPE_HALF, 1) * sa
                     + pltpu.roll(xs, HEAD_W - ROPE_HALF, 1) * sb)
                out_ref[:, seg * SEG_W + c * HEAD_W:seg * SEG_W + (c + 1) * HEAD_W] = y.astype(BF16)
        else:
            out_ref[:, cols] = acc.astype(BF16)

    acc = jnp.dot(u_ref[...], w_ref[:, REF_SEG_AV * SEG_W:(REF_SEG_AV + 1) * SEG_W],
                  preferred_element_type=F32)
    vt_ref[0] = acc.T.astype(BF16)


def _inproj(x2d, norm_g, w_bf, lb, cos_t, sa_t, sb_t, *, tm, rows_per_seq, n_inert):
    n_rows = x2d.shape[0]
    tiles_per_seq = rows_per_seq // tm
    row_spec = pl.BlockSpec((1, D_MODEL), lambda i: (0, 0))
    tab_spec = pl.BlockSpec((tm, HEAD_W), lambda i: (i % tiles_per_seq, 0))
    return pl.pallas_call(
        functools.partial(_inproj_kernel, n_inert),
        grid=(n_rows // tm,),
        in_specs=[pl.BlockSpec((tm, D_MODEL), lambda i: (i, 0)), row_spec,
                  pl.BlockSpec(w_bf.shape, lambda i: (0, 0), pipeline_mode=pl.Buffered(1)),
                  row_spec, tab_spec, tab_spec, tab_spec],
        out_specs=[
            pl.BlockSpec((tm, N_SEG * SEG_W), lambda i: (i, 0)),
            pl.BlockSpec((tm, SEG_W), lambda i: (i, 0)),
            pl.BlockSpec((1, D_MODEL, tm), lambda i: (i, 0, 0)),
        ],
        out_shape=[
            jax.ShapeDtypeStruct((n_rows, N_SEG * SEG_W), BF16),
            jax.ShapeDtypeStruct((n_rows, SEG_W), F32),
            jax.ShapeDtypeStruct((n_rows // tm, D_MODEL, tm), BF16),
        ],
        scratch_shapes=[pltpu.VMEM((tm, D_MODEL), BF16)],
        compiler_params=pltpu.CompilerParams(
            dimension_semantics=("parallel",), vmem_limit_bytes=INPROJ_VMEM_LIMIT),
        name="inproj",
    )(x2d, norm_g, w_bf, lb, cos_t, sa_t, sb_t)


def _bcast_rows(rows, n):
    return jnp.concatenate([jnp.broadcast_to(r, (n, r.shape[1])) for r in rows], axis=0)


def _hgrn_chunk(q, k, v, g, sts, tril):
    n_heads = len(sts)
    width = q.shape[1]
    head = lambda x, h: x[:, h * HEAD_W:(h + 1) * HEAD_W]
    nt = (((1,), (1,)), ((), ()))

    g_hi = g.astype(BF16)
    g_lo = (g - g_hi.astype(F32)).astype(BF16)
    bs = jnp.dot(tril.astype(BF16), jnp.concatenate([g_hi, g_lo], axis=1),
                 preferred_element_type=F32)
    b = bs[:, :width] + bs[:, width:]

    ends = [b[SUB * s + SUB - 1:SUB * s + SUB, :] for s in range(N_SUB)]
    starts = [jnp.zeros_like(ends[0])] + ends[:-1]
    end_b = _bcast_rows(ends, SUB)
    start_b = _bcast_rows(starts, SUB)
    b_last = ends[-1]

    qd = q * jnp.exp2(b - start_b)
    khat = k * jnp.exp2(end_b - b)
    qd_bf = qd.astype(BF16)
    khat_bf = khat.astype(BF16)
    q_s = qd_bf * _bcast_rows([jnp.exp2(s) for s in starts], SUB).astype(BF16)
    k_s = khat_bf * _bcast_rows([jnp.exp2(b_last - e) for e in ends], SUB).astype(BF16)
    decay = jnp.exp2(b_last)

    o_inter = [lax.dot_general(head(q_s, h), sts[h].astype(BF16), nt, preferred_element_type=F32)
               for h in range(n_heads)]
    st_new = [sts[h] * head(decay, h)
              + lax.dot_general(head(v, h), head(k_s, h), (((0,), (0,)), ((), ())),
                                preferred_element_type=F32)
              for h in range(n_heads)]

    a_rows = [[] for _ in range(n_heads)]
    for i in range(N_SUB):
        parts = []
        for jj in range(i + 1):
            kj = khat_bf[SUB * jj:SUB * jj + SUB, :]
            if jj == i:
                w = jnp.exp2(starts[i] - ends[i])
            elif jj == i - 1:
                w = None
            else:
                w = jnp.exp2(starts[i] - ends[jj])
            parts.append(kj if w is None else kj * jnp.broadcast_to(w, kj.shape).astype(BF16))
        if i + 1 < N_SUB:
            parts.append(jnp.zeros((CHUNK - SUB * (i + 1), width), BF16))
        km = jnp.concatenate(parts, axis=0)
        qi = qd_bf[SUB * i:SUB * i + SUB, :]
        for h in range(n_heads):
            a_rows[h].append(lax.dot_general(head(qi, h), head(km, h), nt,
                                             preferred_element_type=F32))
    outs = []
    for h in range(n_heads):
        a = jnp.where(tril, jnp.concatenate(a_rows[h], axis=0), 0.0).astype(BF16)
        outs.append(jnp.dot(a, head(v, h), preferred_element_type=F32) + o_inter[h])
    return outs, st_new


def _hgrn_kernel(pq_ref, pk_ref, pv_ref, pg_ref, q_ref, k_ref, v_ref, z_ref, g_ref, gain_ref, o_ref,
                 st_ref):
    n_chunks = q_ref.shape[0] // CHUNK
    tril = (lax.broadcasted_iota(jnp.int32, (CHUNK, CHUNK), 0)
            >= lax.broadcasted_iota(jnp.int32, (CHUNK, CHUNK), 1))
    gain = gain_ref[...]

    @pl.when(pl.program_id(1) == 0)
    def _():
        _, sts = _hgrn_chunk(pq_ref[...].astype(F32), pk_ref[...].astype(F32), pv_ref[...],
                             pg_ref[...], [jnp.zeros((HEAD_W, HEAD_W), F32)] * N_HEADS, tril)
        for h in range(N_HEADS):
            st_ref[h] = sts[h]

    sts = [st_ref[h] for h in range(N_HEADS)]
    for c in range(n_chunks):
        rows = slice(c * CHUNK, (c + 1) * CHUNK)
        outs, sts = _hgrn_chunk(q_ref[rows, :].astype(F32), k_ref[rows, :].astype(F32),
                                v_ref[rows, :], g_ref[rows, :], sts, tril)
        for h in range(N_HEADS):
            o = outs[h]
            cols = slice(h * HEAD_W, (h + 1) * HEAD_W)
            ms = jnp.mean(o * o, axis=-1, keepdims=True)
            on = o * lax.rsqrt(ms + EPS) * gain
            z = z_ref[rows, cols].astype(F32)
            o_ref[rows, cols] = (on * (z * _sigmoid(z))).astype(BF16)
    for h in range(N_HEADS):
        st_ref[h] = sts[h]


def _hgrn(proj, glog, pproj, pglog, gain, *, batch, seq, tile):
    tiles_per_seq = seq // tile

    def seg_spec(seg):
        return pl.BlockSpec((tile, SEG_W), lambda b, t: (b * tiles_per_seq + t, seg))

    def pseg_spec(seg):
        return pl.BlockSpec((PREFIX, SEG_W), lambda b, t: (0, seg))

    return pl.pallas_call(
        _hgrn_kernel,
        grid=(batch, tiles_per_seq),
        in_specs=[
            pseg_spec(SEG_HQ), pseg_spec(SEG_HF), pseg_spec(SEG_HI), pseg_spec(0),
            seg_spec(SEG_HQ), seg_spec(SEG_HF), seg_spec(SEG_HI), seg_spec(SEG_HZ), seg_spec(0),
            pl.BlockSpec((1, HEAD_W), lambda b, t: (0, 0)),
        ],
        out_specs=seg_spec(0),
        out_shape=jax.ShapeDtypeStruct((batch * seq, SEG_W), BF16),
        scratch_shapes=[pltpu.VMEM((N_HEADS, HEAD_W, HEAD_W), F32)],
        compiler_params=pltpu.CompilerParams(
            dimension_semantics=("parallel", "arbitrary"), vmem_limit_bytes=VMEM_LIMIT),
        name="hgrn2",
    )(pproj, pproj, pproj, pglog, proj, proj, proj, proj, glog, gain)


def _attn_kernel(lam_ref, pk_ref, pvt_ref, q_ref, k_ref, vt_ref, z_ref, gaint_ref, o_ref,
                 acc_ref, qqt_ref, s_ref, m_ref, l_ref):
    bq, bk = ATT_BQ, ATT_BK
    nsub = bq // HEAD_W
    n_q = q_ref.shape[0] // bq
    lam = lam_ref[0, 0]
    steps = [(i, j) for i in range(n_q) for j in range(i + 1)]

    drow = lax.broadcasted_iota(jnp.int32, (HEAD_W, HEAD_W), 0)
    first = drow < DA_D
    pvalid = lax.broadcasted_iota(jnp.int32, (PREFIX, ATT_STRIP), 0) >= N_INERT

    def causal(c):
        kr = key_rows(c)
        krow = lax.broadcasted_iota(jnp.int32, (kr, ATT_STRIP), 0)
        query = lax.broadcasted_iota(jnp.int32, (kr, ATT_STRIP), 1) + (c * ATT_STRIP) % bq
        return krow <= query

    def key_rows(c):
        return min(bk, (c * ATT_STRIP) % bq + ATT_STRIP)

    def build_q(i):
        qts = [q_ref[i * bq + t * HEAD_W:i * bq + (t + 1) * HEAD_W, :].astype(F32).T
               for t in range(nsub)]
        qqt_ref[i % 2] = jnp.concatenate(
            [jnp.where(first, qt, 0.0) for qt in qts] + [jnp.where(first, 0.0, qt) for qt in qts],
            axis=1).astype(BF16)

    def issue_scores(t, c):
        i, j = steps[t]
        cols = slice(c * ATT_STRIP, (c + 1) * ATT_STRIP)
        qs = qqt_ref[i % 2, :, cols]
        kr = key_rows(c) if j == i else bk
        if j == 0:
            keys = jnp.concatenate([pk_ref[...], k_ref[0:kr, :]], axis=0)
            s_ref[t % 2, 0:PREFIX + kr, cols] = jnp.dot(keys, qs, preferred_element_type=F32)
        else:
            s_ref[t % 2, 0:kr, cols] = jnp.dot(k_ref[j * bk:j * bk + kr, :], qs,
                                               preferred_element_type=F32)

    def update(t, c):
        i, j = steps[t]
        par = i % 2
        cols = slice(c * ATT_STRIP, (c + 1) * ATT_STRIP)
        kr = key_rows(c) if j == i else bk
        if j == 0:
            s = s_ref[t % 2, 0:PREFIX + kr, cols]
            body = s[PREFIX:, :]
            if i == 0:
                body = jnp.where(causal(c), body, MASK_VALUE)
            s = jnp.concatenate([jnp.where(pvalid, s[0:PREFIX, :], MASK_VALUE), body], axis=0)
            vtb = jnp.concatenate([pvt_ref[0], vt_ref[0, :, 0:kr]], axis=1)
            m_new = jnp.max(s, axis=0, keepdims=True)
            p = jnp.exp2(s - m_new)
            m_ref[par, :, cols] = m_new
            l_ref[par, :, cols] = jnp.sum(p, axis=0, keepdims=True)
            acc_ref[par, :, cols] = jnp.dot(vtb, p.astype(BF16), preferred_element_type=F32)
        else:
            s = s_ref[t % 2, 0:kr, cols]
            if j == i:
                s = jnp.where(causal(c), s, MASK_VALUE)
            m = m_ref[par, :, cols]
            m_new = jnp.maximum(m, jnp.max(s, axis=0, keepdims=True))
            alpha = jnp.exp2(m - m_new)
            p = jnp.exp2(s - m_new)
            m_ref[par, :, cols] = m_new
            l_ref[par, :, cols] = alpha * l_ref[par, :, cols] + jnp.sum(p, axis=0, keepdims=True)
            acc_ref[par, :, cols] = alpha * acc_ref[par, :, cols] + jnp.dot(
                vt_ref[j, :, 0:kr], p.astype(BF16), preferred_element_type=F32)

    def epilogue(i):
        par = i % 2
        out = acc_ref[par] * (1.0 / l_ref[par])
        ot = out[:, :bq] - lam * out[:, bq:]
        ms = jnp.mean(ot * ot, axis=0, keepdims=True)
        ot = ot * (lax.rsqrt(ms + EPS) * (1.0 - LAM_INIT))
        for t in range(nsub):
            cols = slice(t * HEAD_W, (t + 1) * HEAD_W)
            o = (ot[:, cols] * gaint_ref[...]).T
            rows = slice(i * bq + t * HEAD_W, i * bq + (t + 1) * HEAD_W)
            z = z_ref[rows, :].astype(F32)
            o_ref[rows, :] = (o * (z * _sigmoid(z))).astype(BF16)

    n_strips = 2 * bq // ATT_STRIP
    build_q(0)
    for c in range(n_strips):
        issue_scores(0, c)
    for t, (i, j) in enumerate(steps):
        has_next = t + 1 < len(steps)
        if has_next and steps[t + 1][1] == 0:
            build_q(steps[t + 1][0])
        for c in range(n_strips):
            if has_next:
                issue_scores(t + 1, c)
            update(t, c)
        if j == i:
            epilogue(i)


def _attn(proj, vt, pproj, pvt, lam, gain, *, batch, seq):
    def seg_spec(seg):
        return pl.BlockSpec((seq, HEAD_W), lambda b, h: (b, seg * N_HEADS + h))

    def pseg_spec(seg):
        return pl.BlockSpec((PREFIX, HEAD_W), lambda b, h: (0, seg * N_HEADS + h))

    return pl.pallas_call(
        _attn_kernel,
        grid=(batch, N_HEADS),
        in_specs=[
            pl.BlockSpec(memory_space=pltpu.SMEM),
            pseg_spec(SEG_AK),
            pl.BlockSpec((1, HEAD_W, PREFIX), lambda b, h: (0, h, 0)),
            seg_spec(SEG_AQ), seg_spec(SEG_AK),
            pl.BlockSpec((seq // ATT_BK, HEAD_W, ATT_BK), lambda b, h: (b, h, 0)),
            seg_spec(SEG_AZ),
            pl.BlockSpec((HEAD_W, HEAD_W), lambda b, h: (0, 0)),
        ],
        out_specs=pl.BlockSpec((seq, HEAD_W), lambda b, h: (b, h)),
        out_shape=jax.ShapeDtypeStruct((batch * seq, SEG_W), BF16),
        scratch_shapes=[pltpu.VMEM((2, HEAD_W, 2 * ATT_BQ), F32),
                        pltpu.VMEM((2, HEAD_W, 2 * ATT_BQ), BF16),
                        pltpu.VMEM((2, ATT_BK + PREFIX, 2 * ATT_BQ), F32),
                        pltpu.VMEM((2, 1, 2 * ATT_BQ), F32),
                        pltpu.VMEM((2, 1, 2 * ATT_BQ), F32)],
        compiler_params=pltpu.CompilerParams(
            dimension_semantics=("parallel", "parallel"), vmem_limit_bytes=VMEM_LIMIT),
        name="diff_attn",
    )(lam, pproj, pvt, proj, proj, vt, proj, gain)


def _out_kernel(x_ref, oa_ref, ob_ref, ga_ref, gb_ref, wa_ref, wb_ref, wo_ref, fg_ref, out_ref):
    ya = jnp.dot(oa_ref[...], wa_ref[...], preferred_element_type=F32)
    yb = jnp.dot(ob_ref[...], wb_ref[...], preferred_element_type=F32)
    m = _sigmoid(ga_ref[...].astype(F32)) * ya + _sigmoid(gb_ref[...].astype(F32)) * yb
    h = x_ref[...] + jnp.dot(m.astype(BF16), wo_ref[...], preferred_element_type=F32)
    ms = jnp.mean(h * h, axis=-1, keepdims=True)
    out_ref[...] = h * lax.rsqrt(ms + EPS) * fg_ref[...]


def _out_stage(x2d, oa, ob, proj, wa, wb, wo, fg, *, tm):
    n_rows = x2d.shape[0]
    tile = lambda col: pl.BlockSpec((tm, SEG_W), lambda i: (i, col))
    wspec = pl.BlockSpec((SEG_W, D_MODEL), lambda i: (0, 0))
    return pl.pallas_call(
        _out_kernel,
        grid=(n_rows // tm,),
        in_specs=[tile(0), tile(0), tile(0), tile(SEG_GA), tile(SEG_GB), wspec, wspec, wspec,
                  pl.BlockSpec((1, D_MODEL), lambda i: (0, 0))],
        out_specs=tile(0),
        out_shape=jax.ShapeDtypeStruct((n_rows, D_MODEL), F32),
        compiler_params=pltpu.CompilerParams(
            dimension_semantics=("parallel",), vmem_limit_bytes=VMEM_LIMIT),
        name="out_stage",
    )(x2d, oa, ob, proj, proj, wa, wb, wo, fg)


def _rope_tables(pos):
    lane = jnp.arange(HEAD_W) % DA_D
    inv = ROPE_THETA ** (-(lane % ROPE_HALF).astype(F32) * 2.0 / ROPE_DIM)
    ang = pos.astype(F32)[:, None] * inv[None, :]
    cos, sin = jnp.cos(ang), jnp.sin(ang)
    lo = (lane < ROPE_HALF)[None, :]
    hi = jnp.logical_and(lane >= ROPE_HALF, lane < ROPE_DIM)[None, :]
    cos_t = jnp.where(jnp.logical_or(lo, hi), cos, 1.0)
    sa_t = jnp.where(hi, sin, 0.0)
    sb_t = jnp.where(lo, -sin, 0.0)
    return cos_t, sa_t, sb_t


def kernel(x, meta_tokens, norm_g, w_in, hg_lb_logits, hg_norm_g, da_lambda, da_norm_g,
           w_branch_a, w_branch_b, w_out, final_g):
    batch, seq, d = x.shape
    assert d == D_MODEL and seq % ATT_BQ == 0 and ATT_BQ == ATT_BK
    assert norm_g.shape[0] == 1, "single-layer block"

    x2d = x.reshape(batch * seq, d)
    prefix = jnp.concatenate([jnp.zeros((N_INERT, d), x.dtype), meta_tokens.astype(x.dtype)], axis=0)

    lb = jnp.cumsum(jax.nn.softmax(hg_lb_logits.astype(F32), axis=0), axis=0)[0][None, :]
    lp = da_lambda[0].astype(F32)
    lam = (jnp.exp(jnp.sum(lp[0] * lp[1])) - jnp.exp(jnp.sum(lp[2] * lp[3])) + LAM_INIT).reshape(1, 1)

    w_bf = w_in[0].astype(BF16)
    ng = norm_g[0][None, :].astype(F32)
    tabs_main = _rope_tables(jnp.arange(seq) + N_META)
    tabs_pre = _rope_tables(jnp.maximum(jnp.arange(PREFIX) - N_INERT, 0))

    proj, glog, vt = _inproj(x2d, ng, w_bf, lb, *tabs_main, tm=ATT_BK, rows_per_seq=seq, n_inert=0)
    pproj, pglog, pvt = _inproj(prefix, ng, w_bf, lb, *tabs_pre, tm=PREFIX, rows_per_seq=PREFIX,
                                n_inert=N_INERT)

    oa = _hgrn(proj, glog, pproj, pglog, hg_norm_g[0][None, :].astype(F32), batch=batch, seq=seq,
               tile=512)
    da_gain_rows = jnp.broadcast_to(da_norm_g[0].astype(F32)[:, None], (HEAD_W, HEAD_W))
    ob = _attn(proj, vt, pproj, pvt, lam, da_gain_rows, batch=batch, seq=seq)

    out = _out_stage(x2d, oa, ob, proj, w_branch_a[0].astype(BF16), w_branch_b[0].astype(BF16),
                     w_out[0].astype(BF16), final_g[None, :].astype(F32), tm=1024)
    return out.reshape(batch, seq, d)
```

```python
import functools

import jax
import jax.numpy as jnp
import numpy as np
from jax import lax
from jax.experimental import pallas as pl
from jax.experimental.pallas import tpu as pltpu

F32 = jnp.float32
BF16 = jnp.bfloat16

D_MODEL = 1024
N_HEADS = 8
HEAD_W = 128
DA_D = 64
N_META = 16
PREFIX = 128
N_INERT = PREFIX - N_META
ROPE_DIM = 16
ROPE_HALF = ROPE_DIM // 2
ROPE_THETA = 500000.0
EPS = 1e-6
LAM_INIT = 0.2
MASK_VALUE = -1e30
Q_SCALE = DA_D ** -0.5 * float(np.log2(np.e))

SEG_W = 1024
N_SEG = 9
SEG_HQ, SEG_HF, SEG_HI, SEG_HZ, SEG_AQ, SEG_AK, SEG_AZ, SEG_GA, SEG_GB = range(N_SEG)
REF_SEG_AV = 6

CHUNK = 128
SUB = 16
N_SUB = CHUNK // SUB
ATT_BQ = 512
ATT_BK = 512
ATT_STRIP = 256

VMEM_LIMIT = 56 * 1024 * 1024
INPROJ_VMEM_LIMIT = 58 * 1024 * 1024


def _sigmoid(x):
    return 1.0 / (1.0 + jnp.exp(-x))


def _inproj_kernel(n_inert, x_ref, ng_ref, w_ref, lb_ref, cos_ref, sa_ref, sb_ref,
                   out_ref, glog_ref, vt_ref, u_ref):
    x = x_ref[...]
    ms = jnp.mean(x * x, axis=-1, keepdims=True)
    u_ref[...] = (x * lax.rsqrt(ms + EPS) * ng_ref[...]).astype(BF16)

    for seg in range(N_SEG):
        ref_seg = seg + (seg >= REF_SEG_AV)
        acc = jnp.dot(u_ref[...], w_ref[:, ref_seg * SEG_W:(ref_seg + 1) * SEG_W],
                      preferred_element_type=F32)
        cols = slice(seg * SEG_W, (seg + 1) * SEG_W)
        if seg == SEG_HF:
            lb = lb_ref[...]
            f = lb + (1.0 - lb) * _sigmoid(acc)
            if n_inert:
                ridx = lax.broadcasted_iota(jnp.int32, f.shape, 0)
                f = jnp.where(ridx >= n_inert, f, 1.0)
            glog_ref[...] = jnp.log2(f)
            out_ref[:, cols] = (1.0 - f).astype(BF16)
        elif seg in (SEG_AQ, SEG_AK):
            scale = Q_SCALE if seg == SEG_AQ else 1.0
            cos = cos_ref[...] * scale
            sa = sa_ref[...] * scale
            sb = sb_ref[...] * scale
            for c in range(SEG_W // HEAD_W):
                xs = acc[:, c * HEAD_W:(c + 1) * HEAD_W]
                y = (xs * cos + pltpu.roll(xs, ROPE_HALF, 1) * sa
                     + pltpu.roll(xs, HEAD_W - ROPE_HALF, 1) * sb)
                out_ref[:, seg * SEG_W + c * HEAD_W:seg * SEG_W + (c + 1) * HEAD_W] = y.astype(BF16)
        else:
            out_ref[:, cols] = acc.astype(BF16)

    acc = jnp.dot(u_ref[...], w_ref[:, REF_SEG_AV * SEG_W:(REF_SEG_AV + 1) * SEG_W],
                  preferred_element_type=F32)
    vt_ref[0] = acc.T.astype(BF16)


def _inproj(x2d, norm_g, w_bf, lb, cos_t, sa_t, sb_t, *, tm, rows_per_seq, n_inert):
    n_rows = x2d.shape[0]
    tiles_per_seq = rows_per_seq // tm
    row_spec = pl.BlockSpec((1, D_MODEL), lambda i: (0, 0))
    tab_spec = pl.BlockSpec((tm, HEAD_W), lambda i: (i % tiles_per_seq, 0))
    return pl.pallas_call(
        functools.partial(_inproj_kernel, n_inert),
        grid=(n_rows // tm,),
        in_specs=[pl.BlockSpec((tm, D_MODEL), lambda i: (i, 0)), row_spec,
                  pl.BlockSpec(w_bf.shape, lambda i: (0, 0), pipeline_mode=pl.Buffered(1)),
                  row_spec, tab_spec, tab_spec, tab_spec],
        out_specs=[
            pl.BlockSpec((tm, N_SEG * SEG_W), lambda i: (i, 0)),
            pl.BlockSpec((tm, SEG_W), lambda i: (i, 0)),
            pl.BlockSpec((1, D_MODEL, tm), lambda i: (i, 0, 0)),
        ],
        out_shape=[
            jax.ShapeDtypeStruct((n_rows, N_SEG * SEG_W), BF16),
            jax.ShapeDtypeStruct((n_rows, SEG_W), F32),
            jax.ShapeDtypeStruct((n_rows // tm, D_MODEL, tm), BF16),
        ],
        scratch_shapes=[pltpu.VMEM((tm, D_MODEL), BF16)],
        compiler_params=pltpu.CompilerParams(
            dimension_semantics=("parallel",), vmem_limit_bytes=INPROJ_VMEM_LIMIT),
        name="inproj",
    )(x2d, norm_g, w_bf, lb, cos_t, sa_t, sb_t)


def _bcast_rows(rows, n):
    return jnp.concatenate([jnp.broadcast_to(r, (n, r.shape[1])) for r in rows], axis=0)


def _head(x, h):
    return x[:, h * HEAD_W:(h + 1) * HEAD_W]


_NT = (((1,), (1,)), ((), ()))


def _hgrn_decay(q, k, g, tril):
    width = q.shape[1]
    g_hi = g.astype(BF16)
    g_lo = (g - g_hi.astype(F32)).astype(BF16)
    bs = jnp.dot(tril.astype(BF16), jnp.concatenate([g_hi, g_lo], axis=1),
                 preferred_element_type=F32)
    b = bs[:, :width] + bs[:, width:]

    ends = [b[SUB * s + SUB - 1:SUB * s + SUB, :] for s in range(N_SUB)]
    starts = [jnp.zeros_like(ends[0])] + ends[:-1]
    b_last = ends[-1]
    qd = q * jnp.exp2(b - _bcast_rows(starts, SUB))
    khat = k * jnp.exp2(_bcast_rows(ends, SUB) - b)
    qd_bf = qd.astype(BF16)
    khat_bf = khat.astype(BF16)
    q_s = qd_bf * _bcast_rows([jnp.exp2(s) for s in starts], SUB).astype(BF16)
    k_s = khat_bf * _bcast_rows([jnp.exp2(b_last - e) for e in ends], SUB).astype(BF16)
    return dict(starts=starts, ends=ends, qd=qd_bf, khat=khat_bf, q_s=q_s, k_s=k_s,
                decay=jnp.exp2(b_last))


def _hgrn_scores(d, tril):
    starts, ends, khat_bf, qd_bf = d["starts"], d["ends"], d["khat"], d["qd"]
    width = khat_bf.shape[1]
    a_rows = [[] for _ in range(N_HEADS)]
    for i in range(N_SUB):
        parts = []
        for jj in range(i + 1):
            kj = khat_bf[SUB * jj:SUB * jj + SUB, :]
            if jj == i:
                w = jnp.exp2(starts[i] - ends[i])
            elif jj == i - 1:
                w = None
            else:
                w = jnp.exp2(starts[i] - ends[jj])
            parts.append(kj if w is None else kj * jnp.broadcast_to(w, kj.shape).astype(BF16))
        if i + 1 < N_SUB:
            parts.append(jnp.zeros((CHUNK - SUB * (i + 1), width), BF16))
        km = jnp.concatenate(parts, axis=0)
        qi = qd_bf[SUB * i:SUB * i + SUB, :]
        for h in range(N_HEADS):
            a_rows[h].append(lax.dot_general(_head(qi, h), _head(km, h), _NT,
                                             preferred_element_type=F32))
    return [jnp.where(tril, jnp.concatenate(a_rows[h], axis=0), 0.0).astype(BF16)
            for h in range(N_HEADS)]


def _hgrn_apply(d, a, v, sts):
    outs, st_new = [], []
    for h in range(N_HEADS):
        o_inter = lax.dot_general(_head(d["q_s"], h), sts[h].astype(BF16), _NT,
                                  preferred_element_type=F32)
        st_new.append(sts[h] * _head(d["decay"], h)
                      + lax.dot_general(_head(v, h), _head(d["k_s"], h), (((0,), (0,)), ((), ())),
                                        preferred_element_type=F32))
        outs.append(jnp.dot(a[h], _head(v, h), preferred_element_type=F32) + o_inter)
    return outs, st_new


def _hgrn_kernel(pq_ref, pk_ref, pv_ref, pg_ref, q_ref, k_ref, v_ref, z_ref, g_ref, gain_ref, o_ref,
                 st_ref):
    n_chunks = q_ref.shape[0] // CHUNK
    tril = (lax.broadcasted_iota(jnp.int32, (CHUNK, CHUNK), 0)
            >= lax.broadcasted_iota(jnp.int32, (CHUNK, CHUNK), 1))
    gain = gain_ref[...]

    @pl.when(pl.program_id(1) == 0)
    def _():
        d = _hgrn_decay(pq_ref[...].astype(F32), pk_ref[...].astype(F32), pg_ref[...], tril)
        _, sts = _hgrn_apply(d, _hgrn_scores(d, tril), pv_ref[...],
                             [jnp.zeros((HEAD_W, HEAD_W), F32)] * N_HEADS)
        for h in range(N_HEADS):
            st_ref[h] = sts[h]

    def decay(c):
        rows = slice(c * CHUNK, (c + 1) * CHUNK)
        return _hgrn_decay(q_ref[rows, :].astype(F32), k_ref[rows, :].astype(F32), g_ref[rows, :], tril)

    sts = [st_ref[h] for h in range(N_HEADS)]
    d = decay(0)
    a = _hgrn_scores(d, tril)
    for c in range(n_chunks):
        rows = slice(c * CHUNK, (c + 1) * CHUNK)
        d_next = decay(c + 1) if c + 1 < n_chunks else None
        outs, sts = _hgrn_apply(d, a, v_ref[rows, :], sts)
        a_next = _hgrn_scores(d_next, tril) if d_next is not None else None
        for h in range(N_HEADS):
            o = outs[h]
            cols = slice(h * HEAD_W, (h + 1) * HEAD_W)
            ms = jnp.mean(o * o, axis=-1, keepdims=True)
            on = o * lax.rsqrt(ms + EPS) * gain
            z = z_ref[rows, cols].astype(F32)
            o_ref[rows, cols] = (on * (z * _sigmoid(z))).astype(BF16)
        d, a = d_next, a_next
    for h in range(N_HEADS):
        st_ref[h] = sts[h]


def _hgrn(proj, glog, pproj, pglog, gain, *, batch, seq, tile):
    tiles_per_seq = seq // tile

    def seg_spec(seg):
        return pl.BlockSpec((tile, SEG_W), lambda b, t: (b * tiles_per_seq + t, seg))

    def pseg_spec(seg):
        return pl.BlockSpec((PREFIX, SEG_W), lambda b, t: (0, seg))

    return pl.pallas_call(
        _hgrn_kernel,
        grid=(batch, tiles_per_seq),
        in_specs=[
            pseg_spec(SEG_HQ), pseg_spec(SEG_HF), pseg_spec(SEG_HI), pseg_spec(0),
            seg_spec(SEG_HQ), seg_spec(SEG_HF), seg_spec(SEG_HI), seg_spec(SEG_HZ), seg_spec(0),
            pl.BlockSpec((1, HEAD_W), lambda b, t: (0, 0)),
        ],
        out_specs=seg_spec(0),
        out_shape=jax.ShapeDtypeStruct((batch * seq, SEG_W), BF16),
        scratch_shapes=[pltpu.VMEM((N_HEADS, HEAD_W, HEAD_W), F32)],
        compiler_params=pltpu.CompilerParams(
            dimension_semantics=("parallel", "arbitrary"), vmem_limit_bytes=VMEM_LIMIT),
        name="hgrn2",
    )(pproj, pproj, pproj, pglog, proj, proj, proj, proj, glog, gain)


def _attn_kernel(lam_ref, pk_ref, pvt_ref, q_ref, k_ref, vt_ref, z_ref, gaint_ref, o_ref,
                 acc_ref, qqt_ref, s_ref, m_ref, l_ref):
    bq, bk = ATT_BQ, ATT_BK
    nsub = bq // HEAD_W
    n_q = q_ref.shape[0] // bq
    lam = lam_ref[0, 0]
    steps = [(i, j) for i in range(n_q) for j in range(i + 1)]

    drow = lax.broadcasted_iota(jnp.int32, (HEAD_W, HEAD_W), 0)
    first = drow < DA_D
    pvalid = lax.broadcasted_iota(jnp.int32, (PREFIX, ATT_STRIP), 0) >= N_INERT

    def causal(c):
        kr = key_rows(c)
        krow = lax.broadcasted_iota(jnp.int32, (kr, ATT_STRIP), 0)
        query = lax.broadcasted_iota(jnp.int32, (kr, ATT_STRIP), 1) + (c * ATT_STRIP) % bq
        return krow <= query

    def key_rows(c):
        return min(bk, (c * ATT_STRIP) % bq + ATT_STRIP)

    def build_q(i):
        qts = [q_ref[i * bq + t * HEAD_W:i * bq + (t + 1) * HEAD_W, :].astype(F32).T
               for t in range(nsub)]
        qqt_ref[i % 2] = jnp.concatenate(
            [jnp.where(first, qt, 0.0) for qt in qts] + [jnp.where(first, 0.0, qt) for qt in qts],
            axis=1).astype(BF16)

    def issue_scores(t, c):
        i, j = steps[t]
        cols = slice(c * ATT_STRIP, (c + 1) * ATT_STRIP)
        qs = qqt_ref[i % 2, :, cols]
        kr = key_rows(c) if j == i else bk
        if j == 0:
            keys = jnp.concatenate([pk_ref[...], k_ref[0:kr, :]], axis=0)
            s_ref[t % 2, 0:PREFIX + kr, cols] = jnp.dot(keys, qs, preferred_element_type=F32)
        else:
            s_ref[t % 2, 0:kr, cols] = jnp.dot(k_ref[j * bk:j * bk + kr, :], qs,
                                               preferred_element_type=F32)

    def update(t, c):
        i, j = steps[t]
        par = i % 2
        cols = slice(c * ATT_STRIP, (c + 1) * ATT_STRIP)
        kr = key_rows(c) if j == i else bk
        if j == 0:
            s = s_ref[t % 2, 0:PREFIX + kr, cols]
            body = s[PREFIX:, :]
            if i == 0:
                body = jnp.where(causal(c), body, MASK_VALUE)
            s = jnp.concatenate([jnp.where(pvalid, s[0:PREFIX, :], MASK_VALUE), body], axis=0)
            vtb = jnp.concatenate([pvt_ref[0], vt_ref[0, :, 0:kr]], axis=1)
            m_new = jnp.max(s, axis=0, keepdims=True)
            p = jnp.exp2(s - m_new)
            m_ref[par, :, cols] = m_new
            l_ref[par, :, cols] = jnp.sum(p, axis=0, keepdims=True)
            acc_ref[par, :, cols] = jnp.dot(vtb, p.astype(BF16), preferred_element_type=F32)
        else:
            s = s_ref[t % 2, 0:kr, cols]
            if j == i:
                s = jnp.where(causal(c), s, MASK_VALUE)
            m = m_ref[par, :, cols]
            m_new = jnp.maximum(m, jnp.max(s, axis=0, keepdims=True))
            alpha = jnp.exp2(m - m_new)
            p = jnp.exp2(s - m_new)
            m_ref[par, :, cols] = m_new
            l_ref[par, :, cols] = alpha * l_ref[par, :, cols] + jnp.sum(p, axis=0, keepdims=True)
            acc_ref[par, :, cols] = alpha * acc_ref[par, :, cols] + jnp.dot(
                vt_ref[j, :, 0:kr], p.astype(BF16), preferred_element_type=F32)

    def epilogue(i):
        par = i % 2
        out = acc_ref[par] * (1.0 / l_ref[par])
        ot = out[:, :bq] - lam * out[:, bq:]
        ms = jnp.mean(ot * ot, axis=0, keepdims=True)
        ot = ot * (lax.rsqrt(ms + EPS) * (1.0 - LAM_INIT))
        for t in range(nsub):
            cols = slice(t * HEAD_W, (t + 1) * HEAD_W)
            o = (ot[:, cols] * gaint_ref[...]).T
            rows = slice(i * bq + t * HEAD_W, i * bq + (t + 1) * HEAD_W)
            z = z_ref[rows, :].astype(F32)
            o_ref[rows, :] = (o * (z * _sigmoid(z))).astype(BF16)

    n_strips = 2 * bq // ATT_STRIP
    build_q(0)
    for c in range(n_strips):
        issue_scores(0, c)
    for t, (i, j) in enumerate(steps):
        has_next = t + 1 < len(steps)
        if has_next and steps[t + 1][1] == 0:
            build_q(steps[t + 1][0])
        for c in range(n_strips):
            if has_next:
                issue_scores(t + 1, c)
            update(t, c)
        if j == i:
            epilogue(i)


def _attn(proj, vt, pproj, pvt, lam, gain, *, batch, seq):
    def seg_spec(seg):
        return pl.BlockSpec((seq, HEAD_W), lambda b, h: (b, seg * N_HEADS + h))

    def pseg_spec(seg):
        return pl.BlockSpec((PREFIX, HEAD_W), lambda b, h: (0, seg * N_HEADS + h))

    return pl.pallas_call(
        _attn_kernel,
        grid=(batch, N_HEADS),
        in_specs=[
            pl.BlockSpec(memory_space=pltpu.SMEM),
            pseg_spec(SEG_AK),
            pl.BlockSpec((1, HEAD_W, PREFIX), lambda b, h: (0, h, 0)),
            seg_spec(SEG_AQ), seg_spec(SEG_AK),
            pl.BlockSpec((seq // ATT_BK, HEAD_W, ATT_BK), lambda b, h: (b, h, 0)),
            seg_spec(SEG_AZ),
            pl.BlockSpec((HEAD_W, HEAD_W), lambda b, h: (0, 0)),
        ],
        out_specs=pl.BlockSpec((seq, HEAD_W), lambda b, h: (b, h)),
        out_shape=jax.ShapeDtypeStruct((batch * seq, SEG_W), BF16),
        scratch_shapes=[pltpu.VMEM((2, HEAD_W, 2 * ATT_BQ), F32),
                        pltpu.VMEM((2, HEAD_W, 2 * ATT_BQ), BF16),
                        pltpu.VMEM((2, ATT_BK + PREFIX, 2 * ATT_BQ), F32),
                        pltpu.VMEM((2, 1, 2 * ATT_BQ), F32),
                        pltpu.VMEM((2, 1, 2 * ATT_BQ), F32)],
        compiler_params=pltpu.CompilerParams(
            dimension_semantics=("parallel", "parallel"), vmem_limit_bytes=VMEM_LIMIT),
        name="diff_attn",
    )(lam, pproj, pvt, proj, proj, vt, proj, gain)


def _out_kernel(x_ref, oa_ref, ob_ref, ga_ref, gb_ref, wa_ref, wb_ref, wo_ref, fg_ref, out_ref):
    ya = jnp.dot(oa_ref[...], wa_ref[...], preferred_element_type=F32)
    yb = jnp.dot(ob_ref[...], wb_ref[...], preferred_element_type=F32)
    m = _sigmoid(ga_ref[...].astype(F32)) * ya + _sigmoid(gb_ref[...].astype(F32)) * yb
    h = x_ref[...] + jnp.dot(m.astype(BF16), wo_ref[...], preferred_element_type=F32)
    ms = jnp.mean(h * h, axis=-1, keepdims=True)
    out_ref[...] = h * lax.rsqrt(ms + EPS) * fg_ref[...]


def _out_stage(x2d, oa, ob, proj, wa, wb, wo, fg, *, tm):
    n_rows = x2d.shape[0]
    tile = lambda col: pl.BlockSpec((tm, SEG_W), lambda i: (i, col))
    wspec = pl.BlockSpec((SEG_W, D_MODEL), lambda i: (0, 0))
    return pl.pallas_call(
        _out_kernel,
        grid=(n_rows // tm,),
        in_specs=[tile(0), tile(0), tile(0), tile(SEG_GA), tile(SEG_GB), wspec, wspec, wspec,
                  pl.BlockSpec((1, D_MODEL), lambda i: (0, 0))],
        out_specs=tile(0),
        out_shape=jax.ShapeDtypeStruct((n_rows, D_MODEL), F32),
        compiler_params=pltpu.CompilerParams(
            dimension_semantics=("parallel",), vmem_limit_bytes=VMEM_LIMIT),
        name="out_stage",
    )(x2d, oa, ob, proj, proj, wa, wb, wo, fg)


def _rope_tables(pos):
    lane = jnp.arange(HEAD_W) % DA_D
    inv = ROPE_THETA ** (-(lane % ROPE_HALF).astype(F32) * 2.0 / ROPE_DIM)
    ang = pos.astype(F32)[:, None] * inv[None, :]
    cos, sin = jnp.cos(ang), jnp.sin(ang)
    lo = (lane < ROPE_HALF)[None, :]
    hi = jnp.logical_and(lane >= ROPE_HALF, lane < ROPE_DIM)[None, :]
    cos_t = jnp.where(jnp.logical_or(lo, hi), cos, 1.0)
    sa_t = jnp.where(hi, sin, 0.0)
    sb_t = jnp.where(lo, -sin, 0.0)
    return cos_t, sa_t, sb_t


def kernel(x, meta_tokens, norm_g, w_in, hg_lb_logits, hg_norm_g, da_lambda, da_norm_g,
           w_branch_a, w_branch_b, w_out, final_g):
    batch, seq, d = x.shape
    assert d == D_MODEL and seq % ATT_BQ == 0 and ATT_BQ == ATT_BK
    assert norm_g.shape[0] == 1, "single-layer block"

    x2d = x.reshape(batch * seq, d)
    prefix = jnp.concatenate([jnp.zeros((N_INERT, d), x.dtype), meta_tokens.astype(x.dtype)], axis=0)

    lb = jnp.cumsum(jax.nn.softmax(hg_lb_logits.astype(F32), axis=0), axis=0)[0][None, :]
    lp = da_lambda[0].astype(F32)
    lam = (jnp.exp(jnp.sum(lp[0] * lp[1])) - jnp.exp(jnp.sum(lp[2] * lp[3])) + LAM_INIT).reshape(1, 1)

    w_bf = w_in[0].astype(BF16)
    ng = norm_g[0][None, :].astype(F32)
    tabs_main = _rope_tables(jnp.arange(seq) + N_META)
    tabs_pre = _rope_tables(jnp.maximum(jnp.arange(PREFIX) - N_INERT, 0))

    proj, glog, vt = _inproj(x2d, ng, w_bf, lb, *tabs_main, tm=ATT_BK, rows_per_seq=seq, n_inert=0)
    pproj, pglog, pvt = _inproj(prefix, ng, w_bf, lb, *tabs_pre, tm=PREFIX, rows_per_seq=PREFIX,
                                n_inert=N_INERT)

    oa = _hgrn(proj, glog, pproj, pglog, hg_norm_g[0][None, :].astype(F32), batch=batch, seq=seq,
               tile=1024)
    da_gain_rows = jnp.broadcast_to(da_norm_g[0].astype(F32)[:, None], (HEAD_W, HEAD_W))
    ob = _attn(proj, vt, pproj, pvt, lam, da_gain_rows, batch=batch, seq=seq)

    out = _out_stage(x2d, oa, ob, proj, w_branch_a[0].astype(BF16), w_branch_b[0].astype(BF16),
                     w_out[0].astype(BF16), final_g[None, :].astype(F32), tm=1024)
    return out.reshape(batch, seq, d)
```

```python
import functools

import jax
import jax.numpy as jnp
import numpy as np
from jax import lax
from jax.experimental import pallas as pl
from jax.experimental.pallas import tpu as pltpu

F32 = jnp.float32
BF16 = jnp.bfloat16

D_MODEL = 1024
N_HEADS = 8
HEAD_W = 128
DA_D = 64
N_META = 16
PREFIX = 128
N_INERT = PREFIX - N_META
ROPE_DIM = 16
ROPE_HALF = ROPE_DIM // 2
ROPE_THETA = 500000.0
EPS = 1e-6
LAM_INIT = 0.2
MASK_VALUE = -1e30
Q_SCALE = DA_D ** -0.5 * float(np.log2(np.e))

SEG_W = 1024
N_SEG = 9
SEG_HQ, SEG_HF, SEG_HI, SEG_HZ, SEG_AQ, SEG_AK, SEG_AZ, SEG_GA, SEG_GB = range(N_SEG)
REF_SEG_AV = 6

CHUNK = 128
SUB = 16
N_SUB = CHUNK // SUB
HGRN_SAFE_LOG2_DECAY = 120.0 / SUB
ATT_BQ = 512
ATT_BK = 512
ATT_STRIP = 256

VMEM_LIMIT = 56 * 1024 * 1024
INPROJ_VMEM_LIMIT = 58 * 1024 * 1024


def _sigmoid(x):
    return 1.0 / (1.0 + jnp.exp(-x))


def _inproj_kernel(n_inert, x_ref, ng_ref, w_ref, lb_ref, cos_ref, sa_ref, sb_ref,
                   out_ref, glog_ref, vt_ref, u_ref):
    x = x_ref[...]
    ms = jnp.mean(x * x, axis=-1, keepdims=True)
    u_ref[...] = (x * lax.rsqrt(ms + EPS) * ng_ref[...]).astype(BF16)

    for seg in range(N_SEG):
        ref_seg = seg + (seg >= REF_SEG_AV)
        acc = jnp.dot(u_ref[...], w_ref[:, ref_seg * SEG_W:(ref_seg + 1) * SEG_W],
                      preferred_element_type=F32)
        cols = slice(seg * SEG_W, (seg + 1) * SEG_W)
        if seg == SEG_HF:
            lb = lb_ref[...]
            f = lb + (1.0 - lb) * _sigmoid(acc)
            if n_inert:
                ridx = lax.broadcasted_iota(jnp.int32, f.shape, 0)
                f = jnp.where(ridx >= n_inert, f, 1.0)
            glog_ref[...] = jnp.log2(f)
            out_ref[:, cols] = (1.0 - f).astype(BF16)
        elif seg in (SEG_AQ, SEG_AK):
            scale = Q_SCALE if seg == SEG_AQ else 1.0
            cos = cos_ref[...] * scale
            sa = sa_ref[...] * scale
            sb = sb_ref[...] * scale
            for c in range(SEG_W // HEAD_W):
                xs = acc[:, c * HEAD_W:(c + 1) * HEAD_W]
                y = (xs * cos + pltpu.roll(xs, ROPE_HALF, 1) * sa
                     + pltpu.roll(xs, HEAD_W - ROPE_HALF, 1) * sb)
                out_ref[:, seg * SEG_W + c * HEAD_W:seg * SEG_W + (c + 1) * HEAD_W] = y.astype(BF16)
        else:
            out_ref[:, cols] = acc.astype(BF16)

    acc = jnp.dot(u_ref[...], w_ref[:, REF_SEG_AV * SEG_W:(REF_SEG_AV + 1) * SEG_W],
                  preferred_element_type=F32)
    vt_ref[0] = acc.T.astype(BF16)


def _inproj(x2d, norm_g, w_bf, lb, cos_t, sa_t, sb_t, *, tm, rows_per_seq, n_inert):
    n_rows = x2d.shape[0]
    tiles_per_seq = rows_per_seq // tm
    row_spec = pl.BlockSpec((1, D_MODEL), lambda i: (0, 0))
    tab_spec = pl.BlockSpec((tm, HEAD_W), lambda i: (i % tiles_per_seq, 0))
    return pl.pallas_call(
        functools.partial(_inproj_kernel, n_inert),
        grid=(n_rows // tm,),
        in_specs=[pl.BlockSpec((tm, D_MODEL), lambda i: (i, 0)), row_spec,
                  pl.BlockSpec(w_bf.shape, lambda i: (0, 0), pipeline_mode=pl.Buffered(1)),
                  row_spec, tab_spec, tab_spec, tab_spec],
        out_specs=[
            pl.BlockSpec((tm, N_SEG * SEG_W), lambda i: (i, 0)),
            pl.BlockSpec((tm, SEG_W), lambda i: (i, 0)),
            pl.BlockSpec((1, D_MODEL, tm), lambda i: (i, 0, 0)),
        ],
        out_shape=[
            jax.ShapeDtypeStruct((n_rows, N_SEG * SEG_W), BF16),
            jax.ShapeDtypeStruct((n_rows, SEG_W), F32),
            jax.ShapeDtypeStruct((n_rows // tm, D_MODEL, tm), BF16),
        ],
        scratch_shapes=[pltpu.VMEM((tm, D_MODEL), BF16)],
        compiler_params=pltpu.CompilerParams(
            dimension_semantics=("parallel",), vmem_limit_bytes=INPROJ_VMEM_LIMIT),
        name="inproj",
    )(x2d, norm_g, w_bf, lb, cos_t, sa_t, sb_t)


def _bcast_rows(rows, n):
    return jnp.concatenate([jnp.broadcast_to(r, (n, r.shape[1])) for r in rows], axis=0)


def _head(x, h):
    return x[:, h * HEAD_W:(h + 1) * HEAD_W]


_NT = (((1,), (1,)), ((), ()))


def _hgrn_decay(q, k, g, tril):
    width = q.shape[1]
    g_hi = g.astype(BF16)
    g_lo = (g - g_hi.astype(F32)).astype(BF16)
    bs = jnp.dot(tril.astype(BF16), jnp.concatenate([g_hi, g_lo], axis=1),
                 preferred_element_type=F32)
    b = bs[:, :width] + bs[:, width:]

    ends = [b[SUB * s + SUB - 1:SUB * s + SUB, :] for s in range(N_SUB)]
    starts = [jnp.zeros_like(ends[0])] + ends[:-1]
    b_last = ends[-1]
    qd = q * jnp.exp2(b - _bcast_rows(starts, SUB))
    khat = k * jnp.exp2(_bcast_rows(ends, SUB) - b)
    qd_bf = qd.astype(BF16)
    khat_bf = khat.astype(BF16)
    q_s = qd_bf * _bcast_rows([jnp.exp2(s) for s in starts], SUB).astype(BF16)
    k_s = khat_bf * _bcast_rows([jnp.exp2(b_last - e) for e in ends], SUB).astype(BF16)
    return dict(starts=starts, ends=ends, qd=qd_bf, khat=khat_bf, q_s=q_s, k_s=k_s,
                decay=jnp.exp2(b_last), b=b, q=q, k=k)


def _hgrn_exact_diag(d, i, b_scr, k_scr):
    rows = slice(SUB * i, SUB * i + SUB)
    b_i = d["b"][rows, :]
    q_i = d["q"][rows, :]
    b_scr[...] = b_i
    k_scr[...] = d["k"][rows, :]
    lane = lax.broadcasted_iota(jnp.int32, (SUB, HEAD_W), 1)

    def body(s, accs):
        bs = b_scr[pl.ds(s, 1), :]
        ks = k_scr[pl.ds(s, 1), :]
        z = q_i * (ks * jnp.exp2(jnp.minimum(b_i - bs, 0.0)))
        return tuple(jnp.where(lane == s, jnp.sum(_head(z, h), axis=1, keepdims=True), accs[h])
                     for h in range(N_HEADS))

    return lax.fori_loop(0, SUB, body, tuple(jnp.zeros((SUB, HEAD_W), F32) for _ in range(N_HEADS)))


def _hgrn_scores(d, tril, exact=None):
    starts, ends, khat_bf, qd_bf = d["starts"], d["ends"], d["khat"], d["qd"]
    width = khat_bf.shape[1]
    a_rows = [[] for _ in range(N_HEADS)]
    for i in range(N_SUB):
        parts = []
        for jj in range(i + 1):
            kj = khat_bf[SUB * jj:SUB * jj + SUB, :]
            if jj == i:
                if exact is not None:
                    parts.append(jnp.zeros_like(kj))
                    continue
                w = jnp.exp2(starts[i] - ends[i])
            elif jj == i - 1:
                w = None
            else:
                w = jnp.exp2(starts[i] - ends[jj])
            parts.append(kj if w is None else kj * jnp.broadcast_to(w, kj.shape).astype(BF16))
        if i + 1 < N_SUB:
            parts.append(jnp.zeros((CHUNK - SUB * (i + 1), width), BF16))
        km = jnp.concatenate(parts, axis=0)
        qi = qd_bf[SUB * i:SUB * i + SUB, :]
        diag = _hgrn_exact_diag(d, i, *exact) if exact is not None else None
        for h in range(N_HEADS):
            row = lax.dot_general(_head(qi, h), _head(km, h), _NT, preferred_element_type=F32)
            if diag is not None:
                row = row + (pltpu.roll(diag[h], SUB * i, 1) if i else diag[h])
            a_rows[h].append(row)
    return [jnp.where(tril, jnp.concatenate(a_rows[h], axis=0), 0.0).astype(BF16)
            for h in range(N_HEADS)]


def _hgrn_apply(d, a, v, sts):
    outs, st_new = [], []
    for h in range(N_HEADS):
        o_inter = lax.dot_general(_head(d["q_s"], h), sts[h].astype(BF16), _NT,
                                  preferred_element_type=F32)
        st_new.append(sts[h] * _head(d["decay"], h)
                      + lax.dot_general(_head(v, h), _head(d["k_s"], h), (((0,), (0,)), ((), ())),
                                        preferred_element_type=F32))
        outs.append(jnp.dot(a[h], _head(v, h), preferred_element_type=F32) + o_inter)
    return outs, st_new


def _hgrn_kernel(pq_ref, pk_ref, pv_ref, pg_ref, q_ref, k_ref, v_ref, z_ref, g_ref, gain_ref, o_ref,
                 st_ref, b_scr, k_scr):
    n_chunks = q_ref.shape[0] // CHUNK
    tril = (lax.broadcasted_iota(jnp.int32, (CHUNK, CHUNK), 0)
            >= lax.broadcasted_iota(jnp.int32, (CHUNK, CHUNK), 1))
    gain = gain_ref[...]

    @pl.when(pl.program_id(1) == 0)
    def _():
        d = _hgrn_decay(pq_ref[...].astype(F32), pk_ref[...].astype(F32), pg_ref[...], tril)
        _, sts = _hgrn_apply(d, _hgrn_scores(d, tril, (b_scr, k_scr)), pv_ref[...],
                             [jnp.zeros((HEAD_W, HEAD_W), F32)] * N_HEADS)
        for h in range(N_HEADS):
            st_ref[h] = sts[h]

    def decay(c):
        rows = slice(c * CHUNK, (c + 1) * CHUNK)
        return _hgrn_decay(q_ref[rows, :].astype(F32), k_ref[rows, :].astype(F32), g_ref[rows, :], tril)

    def tile(exact):
        sts = [st_ref[h] for h in range(N_HEADS)]
        d = decay(0)
        a = _hgrn_scores(d, tril, exact)
        for c in range(n_chunks):
            rows = slice(c * CHUNK, (c + 1) * CHUNK)
            d_next = decay(c + 1) if c + 1 < n_chunks else None
            outs, sts = _hgrn_apply(d, a, v_ref[rows, :], sts)
            a_next = _hgrn_scores(d_next, tril, exact) if d_next is not None else None
            for h in range(N_HEADS):
                o = outs[h]
                cols = slice(h * HEAD_W, (h + 1) * HEAD_W)
                ms = jnp.mean(o * o, axis=-1, keepdims=True)
                on = o * lax.rsqrt(ms + EPS) * gain
                z = z_ref[rows, cols].astype(F32)
                o_ref[rows, cols] = (on * (z * _sigmoid(z))).astype(BF16)
            d, a = d_next, a_next
        for h in range(N_HEADS):
            st_ref[h] = sts[h]

    strong_decay = jnp.min(g_ref[...]) < -HGRN_SAFE_LOG2_DECAY

    @pl.when(jnp.logical_not(strong_decay))
    def _():
        tile(None)

    @pl.when(strong_decay)
    def _():
        tile((b_scr, k_scr))


def _hgrn(proj, glog, pproj, pglog, gain, *, batch, seq, tile):
    tiles_per_seq = seq // tile

    def seg_spec(seg):
        return pl.BlockSpec((tile, SEG_W), lambda b, t: (b * tiles_per_seq + t, seg))

    def pseg_spec(seg):
        return pl.BlockSpec((PREFIX, SEG_W), lambda b, t: (0, seg))

    return pl.pallas_call(
        _hgrn_kernel,
        grid=(batch, tiles_per_seq),
        in_specs=[
            pseg_spec(SEG_HQ), pseg_spec(SEG_HF), pseg_spec(SEG_HI), pseg_spec(0),
            seg_spec(SEG_HQ), seg_spec(SEG_HF), seg_spec(SEG_HI), seg_spec(SEG_HZ), seg_spec(0),
            pl.BlockSpec((1, HEAD_W), lambda b, t: (0, 0)),
        ],
        out_specs=seg_spec(0),
        out_shape=jax.ShapeDtypeStruct((batch * seq, SEG_W), BF16),
        scratch_shapes=[pltpu.VMEM((N_HEADS, HEAD_W, HEAD_W), F32),
                        pltpu.VMEM((SUB, SEG_W), F32), pltpu.VMEM((SUB, SEG_W), F32)],
        compiler_params=pltpu.CompilerParams(
            dimension_semantics=("parallel", "arbitrary"), vmem_limit_bytes=VMEM_LIMIT),
        name="hgrn2",
    )(pproj, pproj, pproj, pglog, proj, proj, proj, proj, glog, gain)


def _attn_kernel(lam_ref, pk_ref, pvt_ref, q_ref, k_ref, vt_ref, z_ref, gaint_ref, o_ref,
                 acc_ref, qqt_ref, s_ref, m_ref, l_ref):
    bq, bk = ATT_BQ, ATT_BK
    nsub = bq // HEAD_W
    n_q = q_ref.shape[0] // bq
    lam = lam_ref[0, 0]
    steps = [(i, j) for i in range(n_q) for j in range(i + 1)]

    drow = lax.broadcasted_iota(jnp.int32, (HEAD_W, HEAD_W), 0)
    first = drow < DA_D
    pvalid = lax.broadcasted_iota(jnp.int32, (PREFIX, ATT_STRIP), 0) >= N_INERT

    def causal(c):
        kr = key_rows(c)
        krow = lax.broadcasted_iota(jnp.int32, (kr, ATT_STRIP), 0)
        query = lax.broadcasted_iota(jnp.int32, (kr, ATT_STRIP), 1) + (c * ATT_STRIP) % bq
        return krow <= query

    def key_rows(c):
        return min(bk, (c * ATT_STRIP) % bq + ATT_STRIP)

    def build_q(i):
        qts = [q_ref[i * bq + t * HEAD_W:i * bq + (t + 1) * HEAD_W, :].astype(F32).T
               for t in range(nsub)]
        qqt_ref[i % 2] = jnp.concatenate(
            [jnp.where(first, qt, 0.0) for qt in qts] + [jnp.where(first, 0.0, qt) for qt in qts],
            axis=1).astype(BF16)

    def issue_scores(t, c):
        i, j = steps[t]
        cols = slice(c * ATT_STRIP, (c + 1) * ATT_STRIP)
        qs = qqt_ref[i % 2, :, cols]
        kr = key_rows(c) if j == i else bk
        if j == 0:
            keys = jnp.concatenate([pk_ref[...], k_ref[0:kr, :]], axis=0)
            s_ref[t % 2, 0:PREFIX + kr, cols] = jnp.dot(keys, qs, preferred_element_type=F32)
        else:
            s_ref[t % 2, 0:kr, cols] = jnp.dot(k_ref[j * bk:j * bk + kr, :], qs,
                                               preferred_element_type=F32)

    def update(t, c):
        i, j = steps[t]
        par = i % 2
        cols = slice(c * ATT_STRIP, (c + 1) * ATT_STRIP)
        kr = key_rows(c) if j == i else bk
        if j == 0:
            s = s_ref[t % 2, 0:PREFIX + kr, cols]
            body = s[PREFIX:, :]
            if i == 0:
                body = jnp.where(causal(c), body, MASK_VALUE)
            s = jnp.concatenate([jnp.where(pvalid, s[0:PREFIX, :], MASK_VALUE), body], axis=0)
            vtb = jnp.concatenate([pvt_ref[0], vt_ref[0, :, 0:kr]], axis=1)
            m_new = jnp.max(s, axis=0, keepdims=True)
            p = jnp.exp2(s - m_new)
            m_ref[par, :, cols] = m_new
            l_ref[par, :, cols] = jnp.sum(p, axis=0, keepdims=True)
            acc_ref[par, :, cols] = jnp.dot(vtb, p.astype(BF16), preferred_element_type=F32)
        else:
            s = s_ref[t % 2, 0:kr, cols]
            if j == i:
                s = jnp.where(causal(c), s, MASK_VALUE)
            m = m_ref[par, :, cols]
            m_new = jnp.maximum(m, jnp.max(s, axis=0, keepdims=True))
            alpha = jnp.exp2(m - m_new)
            p = jnp.exp2(s - m_new)
            m_ref[par, :, cols] = m_new
            l_ref[par, :, cols] = alpha * l_ref[par, :, cols] + jnp.sum(p, axis=0, keepdims=True)
            acc_ref[par, :, cols] = alpha * acc_ref[par, :, cols] + jnp.dot(
                vt_ref[j, :, 0:kr], p.astype(BF16), preferred_element_type=F32)

    def epilogue(i):
        par = i % 2
        out = acc_ref[par] * (1.0 / l_ref[par])
        ot = out[:, :bq] - lam * out[:, bq:]
        ms = jnp.mean(ot * ot, axis=0, keepdims=True)
        ot = ot * (lax.rsqrt(ms + EPS) * (1.0 - LAM_INIT))
        for t in range(nsub):
            cols = slice(t * HEAD_W, (t + 1) * HEAD_W)
            o = (ot[:, cols] * gaint_ref[...]).T
            rows = slice(i * bq + t * HEAD_W, i * bq + (t + 1) * HEAD_W)
            z = z_ref[rows, :].astype(F32)
            o_ref[rows, :] = (o * (z * _sigmoid(z))).astype(BF16)

    n_strips = 2 * bq // ATT_STRIP
    build_q(0)
    for c in range(n_strips):
        issue_scores(0, c)
    for t, (i, j) in enumerate(steps):
        has_next = t + 1 < len(steps)
        if has_next and steps[t + 1][1] == 0:
            build_q(steps[t + 1][0])
        for c in range(n_strips):
            if has_next:
                issue_scores(t + 1, c)
            update(t, c)
        if j == i:
            epilogue(i)


def _attn(proj, vt, pproj, pvt, lam, gain, *, batch, seq):
    def seg_spec(seg):
        return pl.BlockSpec((seq, HEAD_W), lambda b, h: (b, seg * N_HEADS + h))

    def pseg_spec(seg):
        return pl.BlockSpec((PREFIX, HEAD_W), lambda b, h: (0, seg * N_HEADS + h))

    return pl.pallas_call(
        _attn_kernel,
        grid=(batch, N_HEADS),
        in_specs=[
            pl.BlockSpec(memory_space=pltpu.SMEM),
            pseg_spec(SEG_AK),
            pl.BlockSpec((1, HEAD_W, PREFIX), lambda b, h: (0, h, 0)),
            seg_spec(SEG_AQ), seg_spec(SEG_AK),
            pl.BlockSpec((seq // ATT_BK, HEAD_W, ATT_BK), lambda b, h: (b, h, 0)),
            seg_spec(SEG_AZ),
            pl.BlockSpec((HEAD_W, HEAD_W), lambda b, h: (0, 0)),
        ],
        out_specs=pl.BlockSpec((seq, HEAD_W), lambda b, h: (b, h)),
        out_shape=jax.ShapeDtypeStruct((batch * seq, SEG_W), BF16),
        scratch_shapes=[pltpu.VMEM((2, HEAD_W, 2 * ATT_BQ), F32),
                        pltpu.VMEM((2, HEAD_W, 2 * ATT_BQ), BF16),
                        pltpu.VMEM((2, ATT_BK + PREFIX, 2 * ATT_BQ), F32),
                        pltpu.VMEM((2, 1, 2 * ATT_BQ), F32),
                        pltpu.VMEM((2, 1, 2 * ATT_BQ), F32)],
        compiler_params=pltpu.CompilerParams(
            dimension_semantics=("parallel", "parallel"), vmem_limit_bytes=VMEM_LIMIT),
        name="diff_attn",
    )(lam, pproj, pvt, proj, proj, vt, proj, gain)


def _out_kernel(x_ref, oa_ref, ob_ref, ga_ref, gb_ref, wa_ref, wb_ref, wo_ref, fg_ref, out_ref):
    ya = jnp.dot(oa_ref[...], wa_ref[...], preferred_element_type=F32)
    yb = jnp.dot(ob_ref[...], wb_ref[...], preferred_element_type=F32)
    m = _sigmoid(ga_ref[...].astype(F32)) * ya + _sigmoid(gb_ref[...].astype(F32)) * yb
    h = x_ref[...] + jnp.dot(m.astype(BF16), wo_ref[...], preferred_element_type=F32)
    ms = jnp.mean(h * h, axis=-1, keepdims=True)
    out_ref[...] = h * lax.rsqrt(ms + EPS) * fg_ref[...]


def _out_stage(x2d, oa, ob, proj, wa, wb, wo, fg, *, tm):
    n_rows = x2d.shape[0]
    tile = lambda col: pl.BlockSpec((tm, SEG_W), lambda i: (i, col))
    wspec = pl.BlockSpec((SEG_W, D_MODEL), lambda i: (0, 0))
    return pl.pallas_call(
        _out_kernel,
        grid=(n_rows // tm,),
        in_specs=[tile(0), tile(0), tile(0), tile(SEG_GA), tile(SEG_GB), wspec, wspec, wspec,
                  pl.BlockSpec((1, D_MODEL), lambda i: (0, 0))],
        out_specs=tile(0),
        out_shape=jax.ShapeDtypeStruct((n_rows, D_MODEL), F32),
        compiler_params=pltpu.CompilerParams(
            dimension_semantics=("parallel",), vmem_limit_bytes=VMEM_LIMIT),
        name="out_stage",
    )(x2d, oa, ob, proj, proj, wa, wb, wo, fg)


def _rope_tables(pos):
    lane = jnp.arange(HEAD_W) % DA_D
    inv = ROPE_THETA ** (-(lane % ROPE_HALF).astype(F32) * 2.0 / ROPE_DIM)
    ang = pos.astype(F32)[:, None] * inv[None, :]
    cos, sin = jnp.cos(ang), jnp.sin(ang)
    lo = (lane < ROPE_HALF)[None, :]
    hi = jnp.logical_and(lane >= ROPE_HALF, lane < ROPE_DIM)[None, :]
    cos_t = jnp.where(jnp.logical_or(lo, hi), cos, 1.0)
    sa_t = jnp.where(hi, sin, 0.0)
    sb_t = jnp.where(lo, -sin, 0.0)
    return cos_t, sa_t, sb_t


def kernel(x, meta_tokens, norm_g, w_in, hg_lb_logits, hg_norm_g, da_lambda, da_norm_g,
           w_branch_a, w_branch_b, w_out, final_g):
    batch, seq, d = x.shape
    assert d == D_MODEL and seq % ATT_BQ == 0 and ATT_BQ == ATT_BK
    assert norm_g.shape[0] == 1, "single-layer block"

    x2d = x.reshape(batch * seq, d)
    prefix = jnp.concatenate([jnp.zeros((N_INERT, d), x.dtype), meta_tokens.astype(x.dtype)], axis=0)

    lb = jnp.cumsum(jax.nn.softmax(hg_lb_logits.astype(F32), axis=0), axis=0)[0][None, :]
    lp = da_lambda[0].astype(F32)
    lam = (jnp.exp(jnp.sum(lp[0] * lp[1])) - jnp.exp(jnp.sum(lp[2] * lp[3])) + LAM_INIT).reshape(1, 1)

    w_bf = w_in[0].astype(BF16)
    ng = norm_g[0][None, :].astype(F32)
    tabs_main = _rope_tables(jnp.arange(seq) + N_META)
    tabs_pre = _rope_tables(jnp.maximum(jnp.arange(PREFIX) - N_INERT, 0))

    proj, glog, vt = _inproj(x2d, ng, w_bf, lb, *tabs_main, tm=ATT_BK, rows_per_seq=seq, n_inert=0)
    pproj, pglog, pvt = _inproj(prefix, ng, w_bf, lb, *tabs_pre, tm=PREFIX, rows_per_seq=PREFIX,
                                n_inert=N_INERT)

    oa = _hgrn(proj, glog, pproj, pglog, hg_norm_g[0][None, :].astype(F32), batch=batch, seq=seq,
               tile=1024)
    da_gain_rows = jnp.broadcast_to(da_norm_g[0].astype(F32)[:, None], (HEAD_W, HEAD_W))
    ob = _attn(proj, vt, pproj, pvt, lam, da_gain_rows, batch=batch, seq=seq)

    out = _out_stage(x2d, oa, ob, proj, w_branch_a[0].astype(BF16), w_branch_b[0].astype(BF16),
                     w_out[0].astype(BF16), final_g[None, :].astype(F32), tm=1024)
    return out.reshape(batch, seq, d)
```

```python
import functools

import jax
import jax.numpy as jnp
import numpy as np
from jax import lax
from jax.experimental import pallas as pl
from jax.experimental.pallas import tpu as pltpu

F32 = jnp.float32
BF16 = jnp.bfloat16

D_MODEL = 1024
N_HEADS = 8
HEAD_W = 128
DA_D = 64
N_META = 16
PREFIX = 128
N_INERT = PREFIX - N_META
ROPE_DIM = 16
ROPE_HALF = ROPE_DIM // 2
ROPE_THETA = 500000.0
EPS = 1e-6
LAM_INIT = 0.2
MASK_VALUE = -1e30
Q_SCALE = DA_D ** -0.5 * float(np.log2(np.e))

SEG_W = 1024
N_SEG = 9
SEG_HQ, SEG_HF, SEG_HI, SEG_HZ, SEG_AQ, SEG_AK, SEG_AZ, SEG_GA, SEG_GB = range(N_SEG)
REF_SEG_AV = 6

CHUNK = 128
SUB = 16
N_SUB = CHUNK // SUB
HGRN_SAFE_LOG2_DECAY = 120.0 / SUB
ATT_BQ = 512
ATT_BK = 512
ATT_STRIP = 256

VMEM_LIMIT = 56 * 1024 * 1024
INPROJ_VMEM_LIMIT = 58 * 1024 * 1024


def _sigmoid(x):
    return 1.0 / (1.0 + jnp.exp(-x))


def _inproj_kernel(n_inert, x_ref, ng_ref, w_ref, lb_ref, cos_ref, sa_ref, sb_ref,
                   out_ref, glog_ref, vt_ref, u_ref):
    x = x_ref[...]
    ms = jnp.mean(x * x, axis=-1, keepdims=True)
    u_ref[...] = (x * lax.rsqrt(ms + EPS) * ng_ref[...]).astype(BF16)

    for seg in range(N_SEG):
        ref_seg = seg + (seg >= REF_SEG_AV)
        acc = jnp.dot(u_ref[...], w_ref[:, ref_seg * SEG_W:(ref_seg + 1) * SEG_W],
                      preferred_element_type=F32)
        cols = slice(seg * SEG_W, (seg + 1) * SEG_W)
        if seg == SEG_HF:
            lb = lb_ref[...]
            f = lb + (1.0 - lb) * _sigmoid(acc)
            if n_inert:
                ridx = lax.broadcasted_iota(jnp.int32, f.shape, 0)
                f = jnp.where(ridx >= n_inert, f, 1.0)
            glog_ref[...] = jnp.log2(f)
            out_ref[:, cols] = (1.0 - f).astype(BF16)
        elif seg in (SEG_AQ, SEG_AK):
            scale = Q_SCALE if seg == SEG_AQ else 1.0
            cos = cos_ref[...] * scale
            sa = sa_ref[...] * scale
            sb = sb_ref[...] * scale
            for c in range(SEG_W // HEAD_W):
                xs = acc[:, c * HEAD_W:(c + 1) * HEAD_W]
                y = (xs * cos + pltpu.roll(xs, ROPE_HALF, 1) * sa
                     + pltpu.roll(xs, HEAD_W - ROPE_HALF, 1) * sb)
                out_ref[:, seg * SEG_W + c * HEAD_W:seg * SEG_W + (c + 1) * HEAD_W] = y.astype(BF16)
        else:
            out_ref[:, cols] = acc.astype(BF16)

    acc = jnp.dot(u_ref[...], w_ref[:, REF_SEG_AV * SEG_W:(REF_SEG_AV + 1) * SEG_W],
                  preferred_element_type=F32)
    vt_ref[0] = acc.T.astype(BF16)


def _inproj(x2d, norm_g, w_bf, lb, cos_t, sa_t, sb_t, *, tm, rows_per_seq, n_inert):
    n_rows = x2d.shape[0]
    tiles_per_seq = rows_per_seq // tm
    row_spec = pl.BlockSpec((1, D_MODEL), lambda i: (0, 0))
    tab_spec = pl.BlockSpec((tm, HEAD_W), lambda i: (i % tiles_per_seq, 0))
    return pl.pallas_call(
        functools.partial(_inproj_kernel, n_inert),
        grid=(n_rows // tm,),
        in_specs=[pl.BlockSpec((tm, D_MODEL), lambda i: (i, 0)), row_spec,
                  pl.BlockSpec(w_bf.shape, lambda i: (0, 0), pipeline_mode=pl.Buffered(1)),
                  row_spec, tab_spec, tab_spec, tab_spec],
        out_specs=[
            pl.BlockSpec((tm, N_SEG * SEG_W), lambda i: (i, 0)),
            pl.BlockSpec((tm, SEG_W), lambda i: (i, 0)),
            pl.BlockSpec((1, D_MODEL, tm), lambda i: (i, 0, 0)),
        ],
        out_shape=[
            jax.ShapeDtypeStruct((n_rows, N_SEG * SEG_W), BF16),
            jax.ShapeDtypeStruct((n_rows, SEG_W), F32),
            jax.ShapeDtypeStruct((n_rows // tm, D_MODEL, tm), BF16),
        ],
        scratch_shapes=[pltpu.VMEM((tm, D_MODEL), BF16)],
        compiler_params=pltpu.CompilerParams(
            dimension_semantics=("parallel",), vmem_limit_bytes=INPROJ_VMEM_LIMIT),
        name="inproj",
    )(x2d, norm_g, w_bf, lb, cos_t, sa_t, sb_t)


def _bcast_rows(rows, n):
    return jnp.concatenate([jnp.broadcast_to(r, (n, r.shape[1])) for r in rows], axis=0)


def _head(x, h):
    return x[:, h * HEAD_W:(h + 1) * HEAD_W]


_NT = (((1,), (1,)), ((), ()))


def _hgrn_decay(q, k, g, tril):
    width = q.shape[1]
    g_hi = g.astype(BF16)
    g_lo = (g - g_hi.astype(F32)).astype(BF16)
    bs = jnp.dot(tril.astype(BF16), jnp.concatenate([g_hi, g_lo], axis=1),
                 preferred_element_type=F32)
    b = bs[:, :width] + bs[:, width:]

    ends = [b[SUB * s + SUB - 1:SUB * s + SUB, :] for s in range(N_SUB)]
    starts = [jnp.zeros_like(ends[0])] + ends[:-1]
    b_last = ends[-1]
    qd = q * jnp.exp2(b - _bcast_rows(starts, SUB))
    khat = k * jnp.exp2(_bcast_rows(ends, SUB) - b)
    qd_bf = qd.astype(BF16)
    khat_bf = khat.astype(BF16)
    q_s = qd_bf * _bcast_rows([jnp.exp2(s) for s in starts], SUB).astype(BF16)
    k_s = khat_bf * _bcast_rows([jnp.exp2(b_last - e) for e in ends], SUB).astype(BF16)
    return dict(starts=starts, ends=ends, qd=qd_bf, khat=khat_bf, q_s=q_s, k_s=k_s,
                decay=jnp.exp2(b_last), b=b, q=q, k=k)


def _hgrn_exact_diag(d, i, b_scr, k_scr):
    rows = slice(SUB * i, SUB * i + SUB)
    b_i = d["b"][rows, :]
    q_i = d["q"][rows, :]
    b_scr[...] = b_i
    k_scr[...] = d["k"][rows, :]
    lane = lax.broadcasted_iota(jnp.int32, (SUB, HEAD_W), 1)

    def body(s, accs):
        bs = b_scr[pl.ds(s, 1), :]
        ks = k_scr[pl.ds(s, 1), :]
        z = q_i * (ks * jnp.exp2(jnp.minimum(b_i - bs, 0.0)))
        return tuple(jnp.where(lane == s, jnp.sum(_head(z, h), axis=1, keepdims=True), accs[h])
                     for h in range(N_HEADS))

    return lax.fori_loop(0, SUB, body, tuple(jnp.zeros((SUB, HEAD_W), F32) for _ in range(N_HEADS)))


def _hgrn_scores(d, tril, exact=None):
    starts, ends, khat_bf, qd_bf = d["starts"], d["ends"], d["khat"], d["qd"]
    width = khat_bf.shape[1]
    a_rows = [[] for _ in range(N_HEADS)]
    for i in range(N_SUB):
        parts = []
        for jj in range(i + 1):
            kj = khat_bf[SUB * jj:SUB * jj + SUB, :]
            if jj == i:
                if exact is not None:
                    parts.append(jnp.zeros_like(kj))
                    continue
                w = jnp.exp2(starts[i] - ends[i])
            elif jj == i - 1:
                w = None
            else:
                w = jnp.exp2(starts[i] - ends[jj])
            parts.append(kj if w is None else kj * jnp.broadcast_to(w, kj.shape).astype(BF16))
        if i + 1 < N_SUB:
            parts.append(jnp.zeros((CHUNK - SUB * (i + 1), width), BF16))
        km = jnp.concatenate(parts, axis=0)
        qi = qd_bf[SUB * i:SUB * i + SUB, :]
        diag = _hgrn_exact_diag(d, i, *exact) if exact is not None else None
        for h in range(N_HEADS):
            row = lax.dot_general(_head(qi, h), _head(km, h), _NT, preferred_element_type=F32)
            if diag is not None:
                row = row + (pltpu.roll(diag[h], SUB * i, 1) if i else diag[h])
            a_rows[h].append(row)
    return [jnp.where(tril, jnp.concatenate(a_rows[h], axis=0), 0.0).astype(BF16)
            for h in range(N_HEADS)]


def _hgrn_apply(d, a, v, sts):
    outs, st_new = [], []
    for h in range(N_HEADS):
        o_inter = lax.dot_general(_head(d["q_s"], h), sts[h].astype(BF16), _NT,
                                  preferred_element_type=F32)
        st_new.append(sts[h] * _head(d["decay"], h)
                      + lax.dot_general(_head(v, h), _head(d["k_s"], h), (((0,), (0,)), ((), ())),
                                        preferred_element_type=F32))
        outs.append(jnp.dot(a[h], _head(v, h), preferred_element_type=F32) + o_inter)
    return outs, st_new


def _min_all(ref):
    rows = ref.shape[0]
    group = max(rows // 8, 8)
    parts = [jnp.min(ref[r:r + group, :], axis=0, keepdims=True) for r in range(0, rows, group)]
    while len(parts) > 1:
        parts = [jnp.minimum(parts[i], parts[i + 1]) for i in range(0, len(parts) - 1, 2)] + (
            [parts[-1]] if len(parts) % 2 else [])
    return jnp.min(parts[0])


def _hgrn_kernel(pq_ref, pk_ref, pv_ref, pg_ref, q_ref, k_ref, v_ref, z_ref, g_ref, gain_ref, o_ref,
                 st_ref, b_scr, k_scr):
    n_chunks = q_ref.shape[0] // CHUNK
    tril = (lax.broadcasted_iota(jnp.int32, (CHUNK, CHUNK), 0)
            >= lax.broadcasted_iota(jnp.int32, (CHUNK, CHUNK), 1))
    gain = gain_ref[...]

    def prefix(exact):
        d = _hgrn_decay(pq_ref[...].astype(F32), pk_ref[...].astype(F32), pg_ref[...], tril)
        _, sts = _hgrn_apply(d, _hgrn_scores(d, tril, exact), pv_ref[...],
                             [jnp.zeros((HEAD_W, HEAD_W), F32)] * N_HEADS)
        for h in range(N_HEADS):
            st_ref[h] = sts[h]

    first_tile = pl.program_id(1) == 0
    strong_prefix_decay = _min_all(pg_ref) < -HGRN_SAFE_LOG2_DECAY

    @pl.when(jnp.logical_and(first_tile, jnp.logical_not(strong_prefix_decay)))
    def _():
        prefix(None)

    @pl.when(jnp.logical_and(first_tile, strong_prefix_decay))
    def _():
        prefix((b_scr, k_scr))

    def decay(c):
        rows = slice(c * CHUNK, (c + 1) * CHUNK)
        return _hgrn_decay(q_ref[rows, :].astype(F32), k_ref[rows, :].astype(F32), g_ref[rows, :], tril)

    def tile(exact):
        sts = [st_ref[h] for h in range(N_HEADS)]
        d = decay(0)
        a = _hgrn_scores(d, tril, exact)
        for c in range(n_chunks):
            rows = slice(c * CHUNK, (c + 1) * CHUNK)
            d_next = decay(c + 1) if c + 1 < n_chunks else None
            outs, sts = _hgrn_apply(d, a, v_ref[rows, :], sts)
            a_next = _hgrn_scores(d_next, tril, exact) if d_next is not None else None
            for h in range(N_HEADS):
                o = outs[h]
                cols = slice(h * HEAD_W, (h + 1) * HEAD_W)
                ms = jnp.mean(o * o, axis=-1, keepdims=True)
                on = o * lax.rsqrt(ms + EPS) * gain
                z = z_ref[rows, cols].astype(F32)
                o_ref[rows, cols] = (on * (z * _sigmoid(z))).astype(BF16)
            d, a = d_next, a_next
        for h in range(N_HEADS):
            st_ref[h] = sts[h]

    strong_decay = _min_all(g_ref) < -HGRN_SAFE_LOG2_DECAY

    @pl.when(jnp.logical_not(strong_decay))
    def _():
        tile(None)

    @pl.when(strong_decay)
    def _():
        tile((b_scr, k_scr))


def _hgrn(proj, glog, pproj, pglog, gain, *, batch, seq, tile):
    tiles_per_seq = seq // tile

    def seg_spec(seg):
        return pl.BlockSpec((tile, SEG_W), lambda b, t: (b * tiles_per_seq + t, seg))

    def pseg_spec(seg):
        return pl.BlockSpec((PREFIX, SEG_W), lambda b, t: (0, seg))

    return pl.pallas_call(
        _hgrn_kernel,
        grid=(batch, tiles_per_seq),
        in_specs=[
            pseg_spec(SEG_HQ), pseg_spec(SEG_HF), pseg_spec(SEG_HI), pseg_spec(0),
            seg_spec(SEG_HQ), seg_spec(SEG_HF), seg_spec(SEG_HI), seg_spec(SEG_HZ), seg_spec(0),
            pl.BlockSpec((1, HEAD_W), lambda b, t: (0, 0)),
        ],
        out_specs=seg_spec(0),
        out_shape=jax.ShapeDtypeStruct((batch * seq, SEG_W), BF16),
        scratch_shapes=[pltpu.VMEM((N_HEADS, HEAD_W, HEAD_W), F32),
                        pltpu.VMEM((SUB, SEG_W), F32), pltpu.VMEM((SUB, SEG_W), F32)],
        compiler_params=pltpu.CompilerParams(
            dimension_semantics=("parallel", "arbitrary"), vmem_limit_bytes=VMEM_LIMIT),
        name="hgrn2",
    )(pproj, pproj, pproj, pglog, proj, proj, proj, proj, glog, gain)


def _attn_kernel(lam_ref, pk_ref, pvt_ref, q_ref, k_ref, vt_ref, z_ref, gaint_ref, o_ref,
                 acc_ref, qqt_ref, s_ref, m_ref, l_ref):
    bq, bk = ATT_BQ, ATT_BK
    nsub = bq // HEAD_W
    n_q = q_ref.shape[0] // bq
    lam = lam_ref[0, 0]
    steps = [(i, j) for i in range(n_q) for j in range(i + 1)]

    drow = lax.broadcasted_iota(jnp.int32, (HEAD_W, HEAD_W), 0)
    first = drow < DA_D
    pvalid = lax.broadcasted_iota(jnp.int32, (PREFIX, ATT_STRIP), 0) >= N_INERT

    def causal(c):
        kr = key_rows(c)
        krow = lax.broadcasted_iota(jnp.int32, (kr, ATT_STRIP), 0)
        query = lax.broadcasted_iota(jnp.int32, (kr, ATT_STRIP), 1) + (c * ATT_STRIP) % bq
        return krow <= query

    def key_rows(c):
        return min(bk, (c * ATT_STRIP) % bq + ATT_STRIP)

    def build_q(i):
        qts = [q_ref[i * bq + t * HEAD_W:i * bq + (t + 1) * HEAD_W, :].astype(F32).T
               for t in range(nsub)]
        qqt_ref[i % 2] = jnp.concatenate(
            [jnp.where(first, qt, 0.0) for qt in qts] + [jnp.where(first, 0.0, qt) for qt in qts],
            axis=1).astype(BF16)

    def issue_scores(t, c):
        i, j = steps[t]
        cols = slice(c * ATT_STRIP, (c + 1) * ATT_STRIP)
        qs = qqt_ref[i % 2, :, cols]
        kr = key_rows(c) if j == i else bk
        if j == 0:
            keys = jnp.concatenate([pk_ref[...], k_ref[0:kr, :]], axis=0)
            s_ref[t % 2, 0:PREFIX + kr, cols] = jnp.dot(keys, qs, preferred_element_type=F32)
        else:
            s_ref[t % 2, 0:kr, cols] = jnp.dot(k_ref[j * bk:j * bk + kr, :], qs,
                                               preferred_element_type=F32)

    def update(t, c):
        i, j = steps[t]
        par = i % 2
        cols = slice(c * ATT_STRIP, (c + 1) * ATT_STRIP)
        kr = key_rows(c) if j == i else bk
        if j == 0:
            s = s_ref[t % 2, 0:PREFIX + kr, cols]
            body = s[PREFIX:, :]
            if i == 0:
                body = jnp.where(causal(c), body, MASK_VALUE)
            s = jnp.concatenate([jnp.where(pvalid, s[0:PREFIX, :], MASK_VALUE), body], axis=0)
            vtb = jnp.concatenate([pvt_ref[0], vt_ref[0, :, 0:kr]], axis=1)
            m_new = jnp.max(s, axis=0, keepdims=True)
            p = jnp.exp2(s - m_new)
            m_ref[par, :, cols] = m_new
            l_ref[par, :, cols] = jnp.sum(p, axis=0, keepdims=True)
            acc_ref[par, :, cols] = jnp.dot(vtb, p.astype(BF16), preferred_element_type=F32)
        else:
            s = s_ref[t % 2, 0:kr, cols]
            if j == i:
                s = jnp.where(causal(c), s, MASK_VALUE)
            m = m_ref[par, :, cols]
            m_new = jnp.maximum(m, jnp.max(s, axis=0, keepdims=True))
            alpha = jnp.exp2(m - m_new)
            p = jnp.exp2(s - m_new)
            m_ref[par, :, cols] = m_new
            l_ref[par, :, cols] = alpha * l_ref[par, :, cols] + jnp.sum(p, axis=0, keepdims=True)
            acc_ref[par, :, cols] = alpha * acc_ref[par, :, cols] + jnp.dot(
                vt_ref[j, :, 0:kr], p.astype(BF16), preferred_element_type=F32)

    def epilogue(i):
        par = i % 2
        out = acc_ref[par] * (1.0 / l_ref[par])
        ot = out[:, :bq] - lam * out[:, bq:]
        ms = jnp.mean(ot * ot, axis=0, keepdims=True)
        ot = ot * (lax.rsqrt(ms + EPS) * (1.0 - LAM_INIT))
        for t in range(nsub):
            cols = slice(t * HEAD_W, (t + 1) * HEAD_W)
            o = (ot[:, cols] * gaint_ref[...]).T
            rows = slice(i * bq + t * HEAD_W, i * bq + (t + 1) * HEAD_W)
            z = z_ref[rows, :].astype(F32)
            o_ref[rows, :] = (o * (z * _sigmoid(z))).astype(BF16)

    n_strips = 2 * bq // ATT_STRIP
    build_q(0)
    for c in range(n_strips):
        issue_scores(0, c)
    for t, (i, j) in enumerate(steps):
        has_next = t + 1 < len(steps)
        if has_next and steps[t + 1][1] == 0:
            build_q(steps[t + 1][0])
        for c in range(n_strips):
            if has_next:
                issue_scores(t + 1, c)
            update(t, c)
        if j == i:
            epilogue(i)


def _attn(proj, vt, pproj, pvt, lam, gain, *, batch, seq):
    def seg_spec(seg):
        return pl.BlockSpec((seq, HEAD_W), lambda b, h: (b, seg * N_HEADS + h))

    def pseg_spec(seg):
        return pl.BlockSpec((PREFIX, HEAD_W), lambda b, h: (0, seg * N_HEADS + h))

    return pl.pallas_call(
        _attn_kernel,
        grid=(batch, N_HEADS),
        in_specs=[
            pl.BlockSpec(memory_space=pltpu.SMEM),
            pseg_spec(SEG_AK),
            pl.BlockSpec((1, HEAD_W, PREFIX), lambda b, h: (0, h, 0)),
            seg_spec(SEG_AQ), seg_spec(SEG_AK),
            pl.BlockSpec((seq // ATT_BK, HEAD_W, ATT_BK), lambda b, h: (b, h, 0)),
            seg_spec(SEG_AZ),
            pl.BlockSpec((HEAD_W, HEAD_W), lambda b, h: (0, 0)),
        ],
        out_specs=pl.BlockSpec((seq, HEAD_W), lambda b, h: (b, h)),
        out_shape=jax.ShapeDtypeStruct((batch * seq, SEG_W), BF16),
        scratch_shapes=[pltpu.VMEM((2, HEAD_W, 2 * ATT_BQ), F32),
                        pltpu.VMEM((2, HEAD_W, 2 * ATT_BQ), BF16),
                        pltpu.VMEM((2, ATT_BK + PREFIX, 2 * ATT_BQ), F32),
                        pltpu.VMEM((2, 1, 2 * ATT_BQ), F32),
                        pltpu.VMEM((2, 1, 2 * ATT_BQ), F32)],
        compiler_params=pltpu.CompilerParams(
            dimension_semantics=("parallel", "parallel"), vmem_limit_bytes=VMEM_LIMIT),
        name="diff_attn",
    )(lam, pproj, pvt, proj, proj, vt, proj, gain)


def _out_kernel(x_ref, oa_ref, ob_ref, ga_ref, gb_ref, wa_ref, wb_ref, wo_ref, fg_ref, out_ref):
    ya = jnp.dot(oa_ref[...], wa_ref[...], preferred_element_type=F32)
    yb = jnp.dot(ob_ref[...], wb_ref[...], preferred_element_type=F32)
    m = _sigmoid(ga_ref[...].astype(F32)) * ya + _sigmoid(gb_ref[...].astype(F32)) * yb
    h = x_ref[...] + jnp.dot(m.astype(BF16), wo_ref[...], preferred_element_type=F32)
    ms = jnp.mean(h * h, axis=-1, keepdims=True)
    out_ref[...] = h * lax.rsqrt(ms + EPS) * fg_ref[...]


def _out_stage(x2d, oa, ob, proj, wa, wb, wo, fg, *, tm):
    n_rows = x2d.shape[0]
    tile = lambda col: pl.BlockSpec((tm, SEG_W), lambda i: (i, col))
    wspec = pl.BlockSpec((SEG_W, D_MODEL), lambda i: (0, 0))
    return pl.pallas_call(
        _out_kernel,
        grid=(n_rows // tm,),
        in_specs=[tile(0), tile(0), tile(0), tile(SEG_GA), tile(SEG_GB), wspec, wspec, wspec,
                  pl.BlockSpec((1, D_MODEL), lambda i: (0, 0))],
        out_specs=tile(0),
        out_shape=jax.ShapeDtypeStruct((n_rows, D_MODEL), F32),
        compiler_params=pltpu.CompilerParams(
            dimension_semantics=("parallel",), vmem_limit_bytes=VMEM_LIMIT),
        name="out_stage",
    )(x2d, oa, ob, proj, proj, wa, wb, wo, fg)


def _rope_tables(pos):
    lane = jnp.arange(HEAD_W) % DA_D
    inv = ROPE_THETA ** (-(lane % ROPE_HALF).astype(F32) * 2.0 / ROPE_DIM)
    ang = pos.astype(F32)[:, None] * inv[None, :]
    cos, sin = jnp.cos(ang), jnp.sin(ang)
    lo = (lane < ROPE_HALF)[None, :]
    hi = jnp.logical_and(lane >= ROPE_HALF, lane < ROPE_DIM)[None, :]
    cos_t = jnp.where(jnp.logical_or(lo, hi), cos, 1.0)
    sa_t = jnp.where(hi, sin, 0.0)
    sb_t = jnp.where(lo, -sin, 0.0)
    return cos_t, sa_t, sb_t


def kernel(x, meta_tokens, norm_g, w_in, hg_lb_logits, hg_norm_g, da_lambda, da_norm_g,
           w_branch_a, w_branch_b, w_out, final_g):
    batch, seq, d = x.shape
    assert d == D_MODEL and seq % ATT_BQ == 0 and ATT_BQ == ATT_BK
    assert norm_g.shape[0] == 1, "single-layer block"

    x2d = x.reshape(batch * seq, d)
    prefix = jnp.concatenate([jnp.zeros((N_INERT, d), x.dtype), meta_tokens.astype(x.dtype)], axis=0)

    lb = jnp.cumsum(jax.nn.softmax(hg_lb_logits.astype(F32), axis=0), axis=0)[0][None, :]
    lp = da_lambda[0].astype(F32)
    lam = (jnp.exp(jnp.sum(lp[0] * lp[1])) - jnp.exp(jnp.sum(lp[2] * lp[3])) + LAM_INIT).reshape(1, 1)

    w_bf = w_in[0].astype(BF16)
    ng = norm_g[0][None, :].astype(F32)
    tabs_main = _rope_tables(jnp.arange(seq) + N_META)
    tabs_pre = _rope_tables(jnp.maximum(jnp.arange(PREFIX) - N_INERT, 0))

    proj, glog, vt = _inproj(x2d, ng, w_bf, lb, *tabs_main, tm=ATT_BK, rows_per_seq=seq, n_inert=0)
    pproj, pglog, pvt = _inproj(prefix, ng, w_bf, lb, *tabs_pre, tm=PREFIX, rows_per_seq=PREFIX,
                                n_inert=N_INERT)

    oa = _hgrn(proj, glog, pproj, pglog, hg_norm_g[0][None, :].astype(F32), batch=batch, seq=seq,
               tile=1024)
    da_gain_rows = jnp.broadcast_to(da_norm_g[0].astype(F32)[:, None], (HEAD_W, HEAD_W))
    ob = _attn(proj, vt, pproj, pvt, lam, da_gain_rows, batch=batch, seq=seq)

    out = _out_stage(x2d, oa, ob, proj, w_branch_a[0].astype(BF16), w_branch_b[0].astype(BF16),
                     w_out[0].astype(BF16), final_g[None, :].astype(F32), tm=1024)
    return out.reshape(batch, seq, d)
```

```python
import functools

import jax
import jax.numpy as jnp
import numpy as np
from jax import lax
from jax.experimental import pallas as pl
from jax.experimental.pallas import tpu as pltpu

F32 = jnp.float32
BF16 = jnp.bfloat16

D_MODEL = 1024
N_HEADS = 8
HEAD_W = 128
DA_D = 64
N_META = 16
PREFIX = 128
N_INERT = PREFIX - N_META
ROPE_DIM = 16
ROPE_HALF = ROPE_DIM // 2
ROPE_THETA = 500000.0
EPS = 1e-6
LAM_INIT = 0.2
MASK_VALUE = -1e30
Q_SCALE = DA_D ** -0.5 * float(np.log2(np.e))

SEG_W = 1024
N_SEG = 9
SEG_HQ, SEG_HF, SEG_HI, SEG_HZ, SEG_AQ, SEG_AK, SEG_AZ, SEG_GA, SEG_GB = range(N_SEG)
REF_SEG_AV = 6

CHUNK = 128
SUB = 16
N_SUB = CHUNK // SUB
HGRN_SAFE_LOG2_DECAY = 120.0 / SUB
ATT_BQ = 512
ATT_BK = 512
ATT_STRIP = 256

VMEM_LIMIT = 56 * 1024 * 1024
INPROJ_VMEM_LIMIT = 58 * 1024 * 1024


def _sigmoid(x):
    return 1.0 / (1.0 + jnp.exp(-x))


def _inproj_kernel(n_inert, x_ref, ng_ref, w_ref, lb_ref, cos_ref, sa_ref, sb_ref,
                   out_ref, glog_ref, gmin_ref, vt_ref, u_ref):
    x = x_ref[...]
    ms = jnp.mean(x * x, axis=-1, keepdims=True)
    u_ref[...] = (x * lax.rsqrt(ms + EPS) * ng_ref[...]).astype(BF16)

    for seg in range(N_SEG):
        ref_seg = seg + (seg >= REF_SEG_AV)
        acc = jnp.dot(u_ref[...], w_ref[:, ref_seg * SEG_W:(ref_seg + 1) * SEG_W],
                      preferred_element_type=F32)
        cols = slice(seg * SEG_W, (seg + 1) * SEG_W)
        if seg == SEG_HF:
            lb = lb_ref[...]
            f = lb + (1.0 - lb) * _sigmoid(acc)
            if n_inert:
                ridx = lax.broadcasted_iota(jnp.int32, f.shape, 0)
                f = jnp.where(ridx >= n_inert, f, 1.0)
            g = jnp.log2(f)
            glog_ref[...] = g
            parts = [g[r:r + 8, :] for r in range(0, g.shape[0], 8)]
            while len(parts) > 1:
                parts = [jnp.minimum(parts[i], parts[i + 1]) for i in range(0, len(parts), 2)]
            gmin_ref[0] = parts[0]
            out_ref[:, cols] = (1.0 - f).astype(BF16)
        elif seg in (SEG_AQ, SEG_AK):
            scale = Q_SCALE if seg == SEG_AQ else 1.0
            cos = cos_ref[...] * scale
            sa = sa_ref[...] * scale
            sb = sb_ref[...] * scale
            for c in range(SEG_W // HEAD_W):
                xs = acc[:, c * HEAD_W:(c + 1) * HEAD_W]
                y = (xs * cos + pltpu.roll(xs, ROPE_HALF, 1) * sa
                     + pltpu.roll(xs, HEAD_W - ROPE_HALF, 1) * sb)
                out_ref[:, seg * SEG_W + c * HEAD_W:seg * SEG_W + (c + 1) * HEAD_W] = y.astype(BF16)
        else:
            out_ref[:, cols] = acc.astype(BF16)

    acc = jnp.dot(u_ref[...], w_ref[:, REF_SEG_AV * SEG_W:(REF_SEG_AV + 1) * SEG_W],
                  preferred_element_type=F32)
    vt_ref[0] = acc.T.astype(BF16)


def _inproj(x2d, norm_g, w_bf, lb, cos_t, sa_t, sb_t, *, tm, rows_per_seq, n_inert):
    n_rows = x2d.shape[0]
    tiles_per_seq = rows_per_seq // tm
    row_spec = pl.BlockSpec((1, D_MODEL), lambda i: (0, 0))
    tab_spec = pl.BlockSpec((tm, HEAD_W), lambda i: (i % tiles_per_seq, 0))
    return pl.pallas_call(
        functools.partial(_inproj_kernel, n_inert),
        grid=(n_rows // tm,),
        in_specs=[pl.BlockSpec((tm, D_MODEL), lambda i: (i, 0)), row_spec,
                  pl.BlockSpec(w_bf.shape, lambda i: (0, 0), pipeline_mode=pl.Buffered(1)),
                  row_spec, tab_spec, tab_spec, tab_spec],
        out_specs=[
            pl.BlockSpec((tm, N_SEG * SEG_W), lambda i: (i, 0)),
            pl.BlockSpec((tm, SEG_W), lambda i: (i, 0)),
            pl.BlockSpec((1, 8, SEG_W), lambda i: (i, 0, 0)),
            pl.BlockSpec((1, D_MODEL, tm), lambda i: (i, 0, 0)),
        ],
        out_shape=[
            jax.ShapeDtypeStruct((n_rows, N_SEG * SEG_W), BF16),
            jax.ShapeDtypeStruct((n_rows, SEG_W), F32),
            jax.ShapeDtypeStruct((n_rows // tm, 8, SEG_W), F32),
            jax.ShapeDtypeStruct((n_rows // tm, D_MODEL, tm), BF16),
        ],
        scratch_shapes=[pltpu.VMEM((tm, D_MODEL), BF16)],
        compiler_params=pltpu.CompilerParams(
            dimension_semantics=("parallel",), vmem_limit_bytes=INPROJ_VMEM_LIMIT),
        name="inproj",
    )(x2d, norm_g, w_bf, lb, cos_t, sa_t, sb_t)


def _bcast_rows(rows, n):
    return jnp.concatenate([jnp.broadcast_to(r, (n, r.shape[1])) for r in rows], axis=0)


def _head(x, h):
    return x[:, h * HEAD_W:(h + 1) * HEAD_W]


_NT = (((1,), (1,)), ((), ()))


def _hgrn_decay(q, k, g, tril):
    width = q.shape[1]
    g_hi = g.astype(BF16)
    g_lo = (g - g_hi.astype(F32)).astype(BF16)
    bs = jnp.dot(tril.astype(BF16), jnp.concatenate([g_hi, g_lo], axis=1),
                 preferred_element_type=F32)
    b = bs[:, :width] + bs[:, width:]

    ends = [b[SUB * s + SUB - 1:SUB * s + SUB, :] for s in range(N_SUB)]
    starts = [jnp.zeros_like(ends[0])] + ends[:-1]
    b_last = ends[-1]
    qd = q * jnp.exp2(b - _bcast_rows(starts, SUB))
    khat = k * jnp.exp2(_bcast_rows(ends, SUB) - b)
    qd_bf = qd.astype(BF16)
    khat_bf = khat.astype(BF16)
    q_s = qd_bf * _bcast_rows([jnp.exp2(s) for s in starts], SUB).astype(BF16)
    k_s = khat_bf * _bcast_rows([jnp.exp2(b_last - e) for e in ends], SUB).astype(BF16)
    return dict(starts=starts, ends=ends, qd=qd_bf, khat=khat_bf, q_s=q_s, k_s=k_s,
                decay=jnp.exp2(b_last), b=b, q=q, k=k)


def _hgrn_exact_diag(d, i, b_scr, k_scr):
    rows = slice(SUB * i, SUB * i + SUB)
    b_i = d["b"][rows, :]
    q_i = d["q"][rows, :]
    b_scr[...] = b_i
    k_scr[...] = d["k"][rows, :]
    lane = lax.broadcasted_iota(jnp.int32, (SUB, HEAD_W), 1)

    def body(s, accs):
        bs = b_scr[pl.ds(s, 1), :]
        ks = k_scr[pl.ds(s, 1), :]
        z = q_i * (ks * jnp.exp2(jnp.minimum(b_i - bs, 0.0)))
        return tuple(jnp.where(lane == s, jnp.sum(_head(z, h), axis=1, keepdims=True), accs[h])
                     for h in range(N_HEADS))

    return lax.fori_loop(0, SUB, body, tuple(jnp.zeros((SUB, HEAD_W), F32) for _ in range(N_HEADS)))


def _hgrn_scores(d, tril, exact=None):
    starts, ends, khat_bf, qd_bf = d["starts"], d["ends"], d["khat"], d["qd"]
    width = khat_bf.shape[1]
    a_rows = [[] for _ in range(N_HEADS)]
    for i in range(N_SUB):
        parts = []
        for jj in range(i + 1):
            kj = khat_bf[SUB * jj:SUB * jj + SUB, :]
            if jj == i:
                if exact is not None:
                    parts.append(jnp.zeros_like(kj))
                    continue
                w = jnp.exp2(starts[i] - ends[i])
            elif jj == i - 1:
                w = None
            else:
                w = jnp.exp2(starts[i] - ends[jj])
            parts.append(kj if w is None else kj * jnp.broadcast_to(w, kj.shape).astype(BF16))
        if i + 1 < N_SUB:
            parts.append(jnp.zeros((CHUNK - SUB * (i + 1), width), BF16))
        km = jnp.concatenate(parts, axis=0)
        qi = qd_bf[SUB * i:SUB * i + SUB, :]
        diag = _hgrn_exact_diag(d, i, *exact) if exact is not None else None
        for h in range(N_HEADS):
            row = lax.dot_general(_head(qi, h), _head(km, h), _NT, preferred_element_type=F32)
            if diag is not None:
                row = row + (pltpu.roll(diag[h], SUB * i, 1) if i else diag[h])
            a_rows[h].append(row)
    return [jnp.where(tril, jnp.concatenate(a_rows[h], axis=0), 0.0).astype(BF16)
            for h in range(N_HEADS)]


def _hgrn_apply(d, a, v, sts):
    outs, st_new = [], []
    for h in range(N_HEADS):
        o_inter = lax.dot_general(_head(d["q_s"], h), sts[h].astype(BF16), _NT,
                                  preferred_element_type=F32)
        st_new.append(sts[h] * _head(d["decay"], h)
                      + lax.dot_general(_head(v, h), _head(d["k_s"], h), (((0,), (0,)), ((), ())),
                                        preferred_element_type=F32))
        outs.append(jnp.dot(a[h], _head(v, h), preferred_element_type=F32) + o_inter)
    return outs, st_new


def _hgrn_kernel(pq_ref, pk_ref, pv_ref, pg_ref, pgmin_ref, q_ref, k_ref, v_ref, z_ref, g_ref, gmin_ref,
                 gain_ref, o_ref, st_ref, b_scr, k_scr):
    n_chunks = q_ref.shape[0] // CHUNK
    tril = (lax.broadcasted_iota(jnp.int32, (CHUNK, CHUNK), 0)
            >= lax.broadcasted_iota(jnp.int32, (CHUNK, CHUNK), 1))
    gain = gain_ref[...]

    def prefix(exact):
        d = _hgrn_decay(pq_ref[...].astype(F32), pk_ref[...].astype(F32), pg_ref[...], tril)
        _, sts = _hgrn_apply(d, _hgrn_scores(d, tril, exact), pv_ref[...],
                             [jnp.zeros((HEAD_W, HEAD_W), F32)] * N_HEADS)
        for h in range(N_HEADS):
            st_ref[h] = sts[h]

    first_tile = pl.program_id(1) == 0
    strong_prefix_decay = jnp.min(pgmin_ref[...]) < -HGRN_SAFE_LOG2_DECAY

    @pl.when(jnp.logical_and(first_tile, jnp.logical_not(strong_prefix_decay)))
    def _():
        prefix(None)

    @pl.when(jnp.logical_and(first_tile, strong_prefix_decay))
    def _():
        prefix((b_scr, k_scr))

    def decay(c):
        rows = slice(c * CHUNK, (c + 1) * CHUNK)
        return _hgrn_decay(q_ref[rows, :].astype(F32), k_ref[rows, :].astype(F32), g_ref[rows, :], tril)

    def tile(exact):
        sts = [st_ref[h] for h in range(N_HEADS)]
        d = decay(0)
        a = _hgrn_scores(d, tril, exact)
        for c in range(n_chunks):
            rows = slice(c * CHUNK, (c + 1) * CHUNK)
            d_next = decay(c + 1) if c + 1 < n_chunks else None
            outs, sts = _hgrn_apply(d, a, v_ref[rows, :], sts)
            a_next = _hgrn_scores(d_next, tril, exact) if d_next is not None else None
            for h in range(N_HEADS):
                o = outs[h]
                cols = slice(h * HEAD_W, (h + 1) * HEAD_W)
                ms = jnp.mean(o * o, axis=-1, keepdims=True)
                on = o * lax.rsqrt(ms + EPS) * gain
                z = z_ref[rows, cols].astype(F32)
                o_ref[rows, cols] = (on * (z * _sigmoid(z))).astype(BF16)
            d, a = d_next, a_next
        for h in range(N_HEADS):
            st_ref[h] = sts[h]

    strong_decay = jnp.min(gmin_ref[...]) < -HGRN_SAFE_LOG2_DECAY

    @pl.when(jnp.logical_not(strong_decay))
    def _():
        tile(None)

    @pl.when(strong_decay)
    def _():
        tile((b_scr, k_scr))


def _hgrn(proj, glog, gmin, pproj, pglog, pgmin, gain, *, batch, seq, tile):
    tiles_per_seq = seq // tile
    gmin_per_tile = gmin.shape[0] * tile // (batch * seq)

    def seg_spec(seg):
        return pl.BlockSpec((tile, SEG_W), lambda b, t: (b * tiles_per_seq + t, seg))

    def pseg_spec(seg):
        return pl.BlockSpec((PREFIX, SEG_W), lambda b, t: (0, seg))

    return pl.pallas_call(
        _hgrn_kernel,
        grid=(batch, tiles_per_seq),
        in_specs=[
            pseg_spec(SEG_HQ), pseg_spec(SEG_HF), pseg_spec(SEG_HI), pseg_spec(0),
            pl.BlockSpec(pgmin.shape, lambda b, t: (0, 0, 0)),
            seg_spec(SEG_HQ), seg_spec(SEG_HF), seg_spec(SEG_HI), seg_spec(SEG_HZ), seg_spec(0),
            pl.BlockSpec((gmin_per_tile, 8, SEG_W), lambda b, t: (b * tiles_per_seq + t, 0, 0)),
            pl.BlockSpec((1, HEAD_W), lambda b, t: (0, 0)),
        ],
        out_specs=seg_spec(0),
        out_shape=jax.ShapeDtypeStruct((batch * seq, SEG_W), BF16),
        scratch_shapes=[pltpu.VMEM((N_HEADS, HEAD_W, HEAD_W), F32),
                        pltpu.VMEM((SUB, SEG_W), F32), pltpu.VMEM((SUB, SEG_W), F32)],
        compiler_params=pltpu.CompilerParams(
            dimension_semantics=("parallel", "arbitrary"), vmem_limit_bytes=VMEM_LIMIT),
        name="hgrn2",
    )(pproj, pproj, pproj, pglog, pgmin, proj, proj, proj, proj, glog, gmin, gain)


def _attn_kernel(lam_ref, pk_ref, pvt_ref, q_ref, k_ref, vt_ref, z_ref, gaint_ref, o_ref,
                 acc_ref, qqt_ref, s_ref, m_ref, l_ref):
    bq, bk = ATT_BQ, ATT_BK
    nsub = bq // HEAD_W
    n_q = q_ref.shape[0] // bq
    lam = lam_ref[0, 0]
    steps = [(i, j) for i in range(n_q) for j in range(i + 1)]

    drow = lax.broadcasted_iota(jnp.int32, (HEAD_W, HEAD_W), 0)
    first = drow < DA_D
    pvalid = lax.broadcasted_iota(jnp.int32, (PREFIX, ATT_STRIP), 0) >= N_INERT

    def causal(c):
        kr = key_rows(c)
        krow = lax.broadcasted_iota(jnp.int32, (kr, ATT_STRIP), 0)
        query = lax.broadcasted_iota(jnp.int32, (kr, ATT_STRIP), 1) + (c * ATT_STRIP) % bq
        return krow <= query

    def key_rows(c):
        return min(bk, (c * ATT_STRIP) % bq + ATT_STRIP)

    def build_q(i):
        qts = [q_ref[i * bq + t * HEAD_W:i * bq + (t + 1) * HEAD_W, :].astype(F32).T
               for t in range(nsub)]
        qqt_ref[i % 2] = jnp.concatenate(
            [jnp.where(first, qt, 0.0) for qt in qts] + [jnp.where(first, 0.0, qt) for qt in qts],
            axis=1).astype(BF16)

    def issue_scores(t, c):
        i, j = steps[t]
        cols = slice(c * ATT_STRIP, (c + 1) * ATT_STRIP)
        qs = qqt_ref[i % 2, :, cols]
        kr = key_rows(c) if j == i else bk
        if j == 0:
            keys = jnp.concatenate([pk_ref[...], k_ref[0:kr, :]], axis=0)
            s_ref[t % 2, 0:PREFIX + kr, cols] = jnp.dot(keys, qs, preferred_element_type=F32)
        else:
            s_ref[t % 2, 0:kr, cols] = jnp.dot(k_ref[j * bk:j * bk + kr, :], qs,
                                               preferred_element_type=F32)

    def update(t, c):
        i, j = steps[t]
        par = i % 2
        cols = slice(c * ATT_STRIP, (c + 1) * ATT_STRIP)
        kr = key_rows(c) if j == i else bk
        if j == 0:
            s = s_ref[t % 2, 0:PREFIX + kr, cols]
            body = s[PREFIX:, :]
            if i == 0:
                body = jnp.where(causal(c), body, MASK_VALUE)
            s = jnp.concatenate([jnp.where(pvalid, s[0:PREFIX, :], MASK_VALUE), body], axis=0)
            vtb = jnp.concatenate([pvt_ref[0], vt_ref[0, :, 0:kr]], axis=1)
            m_new = jnp.max(s, axis=0, keepdims=True)
            p = jnp.exp2(s - m_new)
            m_ref[par, :, cols] = m_new
            l_ref[par, :, cols] = jnp.sum(p, axis=0, keepdims=True)
            acc_ref[par, :, cols] = jnp.dot(vtb, p.astype(BF16), preferred_element_type=F32)
        else:
            s = s_ref[t % 2, 0:kr, cols]
            if j == i:
                s = jnp.where(causal(c), s, MASK_VALUE)
            m = m_ref[par, :, cols]
            m_new = jnp.maximum(m, jnp.max(s, axis=0, keepdims=True))
            alpha = jnp.exp2(m - m_new)
            p = jnp.exp2(s - m_new)
            m_ref[par, :, cols] = m_new
            l_ref[par, :, cols] = alpha * l_ref[par, :, cols] + jnp.sum(p, axis=0, keepdims=True)
            acc_ref[par, :, cols] = alpha * acc_ref[par, :, cols] + jnp.dot(
                vt_ref[j, :, 0:kr], p.astype(BF16), preferred_element_type=F32)

    def epilogue(i):
        par = i % 2
        out = acc_ref[par] * (1.0 / l_ref[par])
        ot = out[:, :bq] - lam * out[:, bq:]
        ms = jnp.mean(ot * ot, axis=0, keepdims=True)
        ot = ot * (lax.rsqrt(ms + EPS) * (1.0 - LAM_INIT))
        for t in range(nsub):
            cols = slice(t * HEAD_W, (t + 1) * HEAD_W)
            o = (ot[:, cols] * gaint_ref[...]).T
            rows = slice(i * bq + t * HEAD_W, i * bq + (t + 1) * HEAD_W)
            z = z_ref[rows, :].astype(F32)
            o_ref[rows, :] = (o * (z * _sigmoid(z))).astype(BF16)

    n_strips = 2 * bq // ATT_STRIP
    build_q(0)
    for c in range(n_strips):
        issue_scores(0, c)
    for t, (i, j) in enumerate(steps):
        has_next = t + 1 < len(steps)
        if has_next and steps[t + 1][1] == 0:
            build_q(steps[t + 1][0])
        for c in range(n_strips):
            if has_next:
                issue_scores(t + 1, c)
            update(t, c)
        if j == i:
            epilogue(i)


def _attn(proj, vt, pproj, pvt, lam, gain, *, batch, seq):
    def seg_spec(seg):
        return pl.BlockSpec((seq, HEAD_W), lambda b, h: (b, seg * N_HEADS + h))

    def pseg_spec(seg):
        return pl.BlockSpec((PREFIX, HEAD_W), lambda b, h: (0, seg * N_HEADS + h))

    return pl.pallas_call(
        _attn_kernel,
        grid=(batch, N_HEADS),
        in_specs=[
            pl.BlockSpec(memory_space=pltpu.SMEM),
            pseg_spec(SEG_AK),
            pl.BlockSpec((1, HEAD_W, PREFIX), lambda b, h: (0, h, 0)),
            seg_spec(SEG_AQ), seg_spec(SEG_AK),
            pl.BlockSpec((seq // ATT_BK, HEAD_W, ATT_BK), lambda b, h: (b, h, 0)),
            seg_spec(SEG_AZ),
            pl.BlockSpec((HEAD_W, HEAD_W), lambda b, h: (0, 0)),
        ],
        out_specs=pl.BlockSpec((seq, HEAD_W), lambda b, h: (b, h)),
        out_shape=jax.ShapeDtypeStruct((batch * seq, SEG_W), BF16),
        scratch_shapes=[pltpu.VMEM((2, HEAD_W, 2 * ATT_BQ), F32),
                        pltpu.VMEM((2, HEAD_W, 2 * ATT_BQ), BF16),
                        pltpu.VMEM((2, ATT_BK + PREFIX, 2 * ATT_BQ), F32),
                        pltpu.VMEM((2, 1, 2 * ATT_BQ), F32),
                        pltpu.VMEM((2, 1, 2 * ATT_BQ), F32)],
        compiler_params=pltpu.CompilerParams(
            dimension_semantics=("parallel", "parallel"), vmem_limit_bytes=VMEM_LIMIT),
        name="diff_attn",
    )(lam, pproj, pvt, proj, proj, vt, proj, gain)


def _out_kernel(x_ref, oa_ref, ob_ref, ga_ref, gb_ref, wa_ref, wb_ref, wo_ref, fg_ref, out_ref):
    ya = jnp.dot(oa_ref[...], wa_ref[...], preferred_element_type=F32)
    yb = jnp.dot(ob_ref[...], wb_ref[...], preferred_element_type=F32)
    m = _sigmoid(ga_ref[...].astype(F32)) * ya + _sigmoid(gb_ref[...].astype(F32)) * yb
    h = x_ref[...] + jnp.dot(m.astype(BF16), wo_ref[...], preferred_element_type=F32)
    ms = jnp.mean(h * h, axis=-1, keepdims=True)
    out_ref[...] = h * lax.rsqrt(ms + EPS) * fg_ref[...]


def _out_stage(x2d, oa, ob, proj, wa, wb, wo, fg, *, tm):
    n_rows = x2d.shape[0]
    tile = lambda col: pl.BlockSpec((tm, SEG_W), lambda i: (i, col))
    wspec = pl.BlockSpec((SEG_W, D_MODEL), lambda i: (0, 0))
    return pl.pallas_call(
        _out_kernel,
        grid=(n_rows // tm,),
        in_specs=[tile(0), tile(0), tile(0), tile(SEG_GA), tile(SEG_GB), wspec, wspec, wspec,
                  pl.BlockSpec((1, D_MODEL), lambda i: (0, 0))],
        out_specs=tile(0),
        out_shape=jax.ShapeDtypeStruct((n_rows, D_MODEL), F32),
        compiler_params=pltpu.CompilerParams(
            dimension_semantics=("parallel",), vmem_limit_bytes=VMEM_LIMIT),
        name="out_stage",
    )(x2d, oa, ob, proj, proj, wa, wb, wo, fg)


def _rope_tables(pos):
    lane = jnp.arange(HEAD_W) % DA_D
    inv = ROPE_THETA ** (-(lane % ROPE_HALF).astype(F32) * 2.0 / ROPE_DIM)
    ang = pos.astype(F32)[:, None] * inv[None, :]
    cos, sin = jnp.cos(ang), jnp.sin(ang)
    lo = (lane < ROPE_HALF)[None, :]
    hi = jnp.logical_and(lane >= ROPE_HALF, lane < ROPE_DIM)[None, :]
    cos_t = jnp.where(jnp.logical_or(lo, hi), cos, 1.0)
    sa_t = jnp.where(hi, sin, 0.0)
    sb_t = jnp.where(lo, -sin, 0.0)
    return cos_t, sa_t, sb_t


def kernel(x, meta_tokens, norm_g, w_in, hg_lb_logits, hg_norm_g, da_lambda, da_norm_g,
           w_branch_a, w_branch_b, w_out, final_g):
    batch, seq, d = x.shape
    assert d == D_MODEL and seq % ATT_BQ == 0 and ATT_BQ == ATT_BK
    assert norm_g.shape[0] == 1, "single-layer block"

    x2d = x.reshape(batch * seq, d)
    prefix = jnp.concatenate([jnp.zeros((N_INERT, d), x.dtype), meta_tokens.astype(x.dtype)], axis=0)

    lb = jnp.cumsum(jax.nn.softmax(hg_lb_logits.astype(F32), axis=0), axis=0)[0][None, :]
    lp = da_lambda[0].astype(F32)
    lam = (jnp.exp(jnp.sum(lp[0] * lp[1])) - jnp.exp(jnp.sum(lp[2] * lp[3])) + LAM_INIT).reshape(1, 1)

    w_bf = w_in[0].astype(BF16)
    ng = norm_g[0][None, :].astype(F32)
    tabs_main = _rope_tables(jnp.arange(seq) + N_META)
    tabs_pre = _rope_tables(jnp.maximum(jnp.arange(PREFIX) - N_INERT, 0))

    proj, glog, gmin, vt = _inproj(x2d, ng, w_bf, lb, *tabs_main, tm=ATT_BK, rows_per_seq=seq,
                                   n_inert=0)
    pproj, pglog, pgmin, pvt = _inproj(prefix, ng, w_bf, lb, *tabs_pre, tm=PREFIX,
                                       rows_per_seq=PREFIX, n_inert=N_INERT)

    oa = _hgrn(proj, glog, gmin, pproj, pglog, pgmin, hg_norm_g[0][None, :].astype(F32),
               batch=batch, seq=seq, tile=1024)
    da_gain_rows = jnp.broadcast_to(da_norm_g[0].astype(F32)[:, None], (HEAD_W, HEAD_W))
    ob = _attn(proj, vt, pproj, pvt, lam, da_gain_rows, batch=batch, seq=seq)

    out = _out_stage(x2d, oa, ob, proj, w_branch_a[0].astype(BF16), w_branch_b[0].astype(BF16),
                     w_out[0].astype(BF16), final_g[None, :].astype(F32), tm=1024)
    return out.reshape(batch, seq, d)
```

```python
import functools

import jax
import jax.numpy as jnp
import numpy as np
from jax import lax
from jax.experimental import pallas as pl
from jax.experimental.pallas import tpu as pltpu

F32 = jnp.float32
BF16 = jnp.bfloat16

D_MODEL = 1024
N_HEADS = 8
HEAD_W = 128
DA_D = 64
N_META = 16
PREFIX = 128
N_INERT = PREFIX - N_META
ROPE_DIM = 16
ROPE_HALF = ROPE_DIM // 2
ROPE_THETA = 500000.0
EPS = 1e-6
LAM_INIT = 0.2
MASK_VALUE = -1e30
Q_SCALE = DA_D ** -0.5 * float(np.log2(np.e))

SEG_W = 1024
N_SEG = 9
SEG_HQ, SEG_HF, SEG_HI, SEG_HZ, SEG_AQ, SEG_AK, SEG_AZ, SEG_GA, SEG_GB = range(N_SEG)
REF_SEG_AV = 6

CHUNK = 128
SUB = 16
N_SUB = CHUNK // SUB
HGRN_SAFE_LOG2_DECAY = 120.0 / SUB
ATT_BQ = 512
ATT_BK = 512
ATT_STRIP = 256

VMEM_LIMIT = 56 * 1024 * 1024
INPROJ_VMEM_LIMIT = 58 * 1024 * 1024


def _sigmoid(x):
    return 1.0 / (1.0 + jnp.exp(-x))


def _inproj_kernel(n_inert, x_ref, ng_ref, w_ref, lb_ref, cos_ref, sa_ref, sb_ref,
                   out_ref, glog_ref, gmin_ref, vt_ref, u_ref):
    x = x_ref[...]
    ms = jnp.mean(x * x, axis=-1, keepdims=True)
    u_ref[...] = (x * lax.rsqrt(ms + EPS) * ng_ref[...]).astype(BF16)

    for seg in range(N_SEG):
        ref_seg = seg + (seg >= REF_SEG_AV)
        acc = jnp.dot(u_ref[...], w_ref[:, ref_seg * SEG_W:(ref_seg + 1) * SEG_W],
                      preferred_element_type=F32)
        cols = slice(seg * SEG_W, (seg + 1) * SEG_W)
        if seg == SEG_HF:
            lb = lb_ref[...]
            f = lb + (1.0 - lb) * _sigmoid(acc)
            if n_inert:
                ridx = lax.broadcasted_iota(jnp.int32, f.shape, 0)
                f = jnp.where(ridx >= n_inert, f, 1.0)
            g = jnp.log2(f)
            glog_ref[...] = g
            parts = [g[r:r + 8, :] for r in range(0, g.shape[0], 8)]
            while len(parts) > 1:
                parts = [jnp.minimum(parts[i], parts[i + 1]) for i in range(0, len(parts), 2)]
            gmin_ref[0] = parts[0]
            out_ref[:, cols] = (1.0 - f).astype(BF16)
        elif seg in (SEG_AQ, SEG_AK):
            scale = Q_SCALE if seg == SEG_AQ else 1.0
            cos = cos_ref[...] * scale
            sa = sa_ref[...] * scale
            sb = sb_ref[...] * scale
            for c in range(SEG_W // HEAD_W):
                xs = acc[:, c * HEAD_W:(c + 1) * HEAD_W]
                y = (xs * cos + pltpu.roll(xs, ROPE_HALF, 1) * sa
                     + pltpu.roll(xs, HEAD_W - ROPE_HALF, 1) * sb)
                out_ref[:, seg * SEG_W + c * HEAD_W:seg * SEG_W + (c + 1) * HEAD_W] = y.astype(BF16)
        else:
            out_ref[:, cols] = acc.astype(BF16)

    acc = jnp.dot(u_ref[...], w_ref[:, REF_SEG_AV * SEG_W:(REF_SEG_AV + 1) * SEG_W],
                  preferred_element_type=F32)
    vt_ref[0] = acc.T.astype(BF16)


def _inproj(x2d, norm_g, w_bf, lb, cos_t, sa_t, sb_t, *, tm, rows_per_seq, n_inert):
    n_rows = x2d.shape[0]
    tiles_per_seq = rows_per_seq // tm
    row_spec = pl.BlockSpec((1, D_MODEL), lambda i: (0, 0))
    tab_spec = pl.BlockSpec((tm, HEAD_W), lambda i: (i % tiles_per_seq, 0))
    return pl.pallas_call(
        functools.partial(_inproj_kernel, n_inert),
        grid=(n_rows // tm,),
        in_specs=[pl.BlockSpec((tm, D_MODEL), lambda i: (i, 0)), row_spec,
                  pl.BlockSpec(w_bf.shape, lambda i: (0, 0), pipeline_mode=pl.Buffered(1)),
                  row_spec, tab_spec, tab_spec, tab_spec],
        out_specs=[
            pl.BlockSpec((tm, N_SEG * SEG_W), lambda i: (i, 0)),
            pl.BlockSpec((tm, SEG_W), lambda i: (i, 0)),
            pl.BlockSpec((1, 8, SEG_W), lambda i: (i, 0, 0)),
            pl.BlockSpec((1, D_MODEL, tm), lambda i: (i, 0, 0)),
        ],
        out_shape=[
            jax.ShapeDtypeStruct((n_rows, N_SEG * SEG_W), BF16),
            jax.ShapeDtypeStruct((n_rows, SEG_W), F32),
            jax.ShapeDtypeStruct((n_rows // tm, 8, SEG_W), F32),
            jax.ShapeDtypeStruct((n_rows // tm, D_MODEL, tm), BF16),
        ],
        scratch_shapes=[pltpu.VMEM((tm, D_MODEL), BF16)],
        compiler_params=pltpu.CompilerParams(
            dimension_semantics=("parallel",), vmem_limit_bytes=INPROJ_VMEM_LIMIT),
        name="inproj",
    )(x2d, norm_g, w_bf, lb, cos_t, sa_t, sb_t)


def _bcast_rows(rows, n):
    return jnp.concatenate([jnp.broadcast_to(r, (n, r.shape[1])) for r in rows], axis=0)


def _head(x, h):
    return x[:, h * HEAD_W:(h + 1) * HEAD_W]


_NT = (((1,), (1,)), ((), ()))


def _hgrn_decay(q, k, g, tril):
    width = q.shape[1]
    g_hi = g.astype(BF16)
    g_lo = (g - g_hi.astype(F32)).astype(BF16)
    bs = jnp.dot(tril.astype(BF16), jnp.concatenate([g_hi, g_lo], axis=1),
                 preferred_element_type=F32)
    b = bs[:, :width] + bs[:, width:]

    ends = [b[SUB * s + SUB - 1:SUB * s + SUB, :] for s in range(N_SUB)]
    starts = [jnp.zeros_like(ends[0])] + ends[:-1]
    b_last = ends[-1]
    qd = q * jnp.exp2(b - _bcast_rows(starts, SUB))
    khat = k * jnp.exp2(_bcast_rows(ends, SUB) - b)
    qd_bf = qd.astype(BF16)
    khat_bf = khat.astype(BF16)
    q_s = qd_bf * _bcast_rows([jnp.exp2(s) for s in starts], SUB).astype(BF16)
    k_s = khat_bf * _bcast_rows([jnp.exp2(b_last - e) for e in ends], SUB).astype(BF16)
    return dict(starts=starts, ends=ends, qd=qd_bf, khat=khat_bf, q_s=q_s, k_s=k_s,
                decay=jnp.exp2(b_last), b=b, q=q, k=k)


def _hgrn_exact_diag(d, i, b_scr, k_scr):
    rows = slice(SUB * i, SUB * i + SUB)
    b_i = d["b"][rows, :]
    q_i = d["q"][rows, :]
    b_scr[...] = b_i
    k_scr[...] = d["k"][rows, :]
    lane = lax.broadcasted_iota(jnp.int32, (SUB, HEAD_W), 1)

    def body(s, accs):
        bs = b_scr[pl.ds(s, 1), :]
        ks = k_scr[pl.ds(s, 1), :]
        z = q_i * (ks * jnp.exp2(jnp.minimum(b_i - bs, 0.0)))
        return tuple(jnp.where(lane == s, jnp.sum(_head(z, h), axis=1, keepdims=True), accs[h])
                     for h in range(N_HEADS))

    return lax.fori_loop(0, SUB, body, tuple(jnp.zeros((SUB, HEAD_W), F32) for _ in range(N_HEADS)))


def _hgrn_scores(d, tril, exact=None):
    starts, ends, khat_bf, qd_bf = d["starts"], d["ends"], d["khat"], d["qd"]
    width = khat_bf.shape[1]
    a_rows = [[] for _ in range(N_HEADS)]
    for i in range(N_SUB):
        parts = []
        for jj in range(i + 1):
            kj = khat_bf[SUB * jj:SUB * jj + SUB, :]
            if jj == i:
                if exact is not None:
                    parts.append(jnp.zeros_like(kj))
                    continue
                w = jnp.exp2(starts[i] - ends[i])
            elif jj == i - 1:
                w = None
            else:
                w = jnp.exp2(starts[i] - ends[jj])
            parts.append(kj if w is None else kj * jnp.broadcast_to(w, kj.shape).astype(BF16))
        if i + 1 < N_SUB:
            parts.append(jnp.zeros((CHUNK - SUB * (i + 1), width), BF16))
        km = jnp.concatenate(parts, axis=0)
        qi = qd_bf[SUB * i:SUB * i + SUB, :]
        diag = _hgrn_exact_diag(d, i, *exact) if exact is not None else None
        for h in range(N_HEADS):
            row = lax.dot_general(_head(qi, h), _head(km, h), _NT, preferred_element_type=F32)
            if diag is not None:
                row = row + (pltpu.roll(diag[h], SUB * i, 1) if i else diag[h])
            a_rows[h].append(row)
    return [jnp.where(tril, jnp.concatenate(a_rows[h], axis=0), 0.0).astype(BF16)
            for h in range(N_HEADS)]


def _hgrn_apply(d, a, v, sts):
    outs, st_new = [], []
    for h in range(N_HEADS):
        o_inter = lax.dot_general(_head(d["q_s"], h), sts[h].astype(BF16), _NT,
                                  preferred_element_type=F32)
        st_new.append(sts[h] * _head(d["decay"], h)
                      + lax.dot_general(_head(v, h), _head(d["k_s"], h), (((0,), (0,)), ((), ())),
                                        preferred_element_type=F32))
        outs.append(jnp.dot(a[h], _head(v, h), preferred_element_type=F32) + o_inter)
    return outs, st_new


def _hgrn_kernel(pq_ref, pk_ref, pv_ref, pg_ref, pgmin_ref, q_ref, k_ref, v_ref, z_ref, g_ref, gmin_ref,
                 gain_ref, o_ref, st_ref, b_scr, k_scr):
    n_chunks = q_ref.shape[0] // CHUNK
    tril = (lax.broadcasted_iota(jnp.int32, (CHUNK, CHUNK), 0)
            >= lax.broadcasted_iota(jnp.int32, (CHUNK, CHUNK), 1))
    gain = gain_ref[...]

    def prefix(exact):
        d = _hgrn_decay(pq_ref[...].astype(F32), pk_ref[...].astype(F32), pg_ref[...], tril)
        _, sts = _hgrn_apply(d, _hgrn_scores(d, tril, exact), pv_ref[...],
                             [jnp.zeros((HEAD_W, HEAD_W), F32)] * N_HEADS)
        for h in range(N_HEADS):
            st_ref[h] = sts[h]

    first_tile = pl.program_id(1) == 0
    strong_prefix_decay = jnp.min(pgmin_ref[...]) < -HGRN_SAFE_LOG2_DECAY

    @pl.when(jnp.logical_and(first_tile, jnp.logical_not(strong_prefix_decay)))
    def _():
        prefix(None)

    @pl.when(jnp.logical_and(first_tile, strong_prefix_decay))
    def _():
        prefix((b_scr, k_scr))

    def decay(c):
        rows = slice(c * CHUNK, (c + 1) * CHUNK)
        return _hgrn_decay(q_ref[rows, :].astype(F32), k_ref[rows, :].astype(F32), g_ref[rows, :], tril)

    def tile(exact):
        sts = [st_ref[h] for h in range(N_HEADS)]
        d = decay(0)
        a = _hgrn_scores(d, tril, exact)
        for c in range(n_chunks):
            rows = slice(c * CHUNK, (c + 1) * CHUNK)
            d_next = decay(c + 1) if c + 1 < n_chunks else None
            outs, sts = _hgrn_apply(d, a, v_ref[rows, :], sts)
            a_next = _hgrn_scores(d_next, tril, exact) if d_next is not None else None
            for h in range(N_HEADS):
                o = outs[h]
                cols = slice(h * HEAD_W, (h + 1) * HEAD_W)
                ms = jnp.mean(o * o, axis=-1, keepdims=True)
                on = o * lax.rsqrt(ms + EPS) * gain
                z = z_ref[rows, cols].astype(F32)
                o_ref[rows, cols] = (on * (z * _sigmoid(z))).astype(BF16)
            d, a = d_next, a_next
        for h in range(N_HEADS):
            st_ref[h] = sts[h]

    strong_decay = jnp.min(gmin_ref[...]) < -HGRN_SAFE_LOG2_DECAY

    @pl.when(jnp.logical_not(strong_decay))
    def _():
        tile(None)

    @pl.when(strong_decay)
    def _():
        tile((b_scr, k_scr))


def _hgrn(proj, glog, gmin, pproj, pglog, pgmin, gain, *, batch, seq, tile):
    tiles_per_seq = seq // tile
    gmin_per_tile = gmin.shape[0] * tile // (batch * seq)

    def seg_spec(seg):
        return pl.BlockSpec((tile, SEG_W), lambda b, t: (b * tiles_per_seq + t, seg))

    def pseg_spec(seg):
        return pl.BlockSpec((PREFIX, SEG_W), lambda b, t: (0, seg))

    return pl.pallas_call(
        _hgrn_kernel,
        grid=(batch, tiles_per_seq),
        in_specs=[
            pseg_spec(SEG_HQ), pseg_spec(SEG_HF), pseg_spec(SEG_HI), pseg_spec(0),
            pl.BlockSpec(pgmin.shape, lambda b, t: (0, 0, 0)),
            seg_spec(SEG_HQ), seg_spec(SEG_HF), seg_spec(SEG_HI), seg_spec(SEG_HZ), seg_spec(0),
            pl.BlockSpec((gmin_per_tile, 8, SEG_W), lambda b, t: (b * tiles_per_seq + t, 0, 0)),
            pl.BlockSpec((1, HEAD_W), lambda b, t: (0, 0)),
        ],
        out_specs=seg_spec(0),
        out_shape=jax.ShapeDtypeStruct((batch * seq, SEG_W), BF16),
        scratch_shapes=[pltpu.VMEM((N_HEADS, HEAD_W, HEAD_W), F32),
                        pltpu.VMEM((SUB, SEG_W), F32), pltpu.VMEM((SUB, SEG_W), F32)],
        compiler_params=pltpu.CompilerParams(
            dimension_semantics=("parallel", "arbitrary"), vmem_limit_bytes=VMEM_LIMIT),
        name="hgrn2",
    )(pproj, pproj, pproj, pglog, pgmin, proj, proj, proj, proj, glog, gmin, gain)


def _attn_kernel(lam_ref, pk_ref, pvt_ref, q_ref, k_ref, vt_ref, z_ref, gaint_ref, o_ref,
                 acc_ref, qqt_ref, s_ref, m_ref, l_ref):
    bq, bk = ATT_BQ, ATT_BK
    nsub = bq // HEAD_W
    n_q = q_ref.shape[0] // bq
    lam = lam_ref[0, 0]
    steps = [(i, j) for i in range(n_q) for j in range(i + 1)]

    drow = lax.broadcasted_iota(jnp.int32, (HEAD_W, HEAD_W), 0)
    first = drow < DA_D
    pvalid = lax.broadcasted_iota(jnp.int32, (PREFIX, ATT_STRIP), 0) >= N_INERT

    def causal(c):
        kr = key_rows(c)
        krow = lax.broadcasted_iota(jnp.int32, (kr, ATT_STRIP), 0)
        query = lax.broadcasted_iota(jnp.int32, (kr, ATT_STRIP), 1) + (c * ATT_STRIP) % bq
        return krow <= query

    def key_rows(c):
        return min(bk, (c * ATT_STRIP) % bq + ATT_STRIP)

    def build_q(i):
        qts = [q_ref[i * bq + t * HEAD_W:i * bq + (t + 1) * HEAD_W, :].astype(F32).T
               for t in range(nsub)]
        qqt_ref[i % 2] = jnp.concatenate(
            [jnp.where(first, qt, 0.0) for qt in qts] + [jnp.where(first, 0.0, qt) for qt in qts],
            axis=1).astype(BF16)

    def issue_scores(t, c):
        i, j = steps[t]
        cols = slice(c * ATT_STRIP, (c + 1) * ATT_STRIP)
        qs = qqt_ref[i % 2, :, cols]
        kr = key_rows(c) if j == i else bk
        if j == 0:
            keys = jnp.concatenate([pk_ref[...], k_ref[0:kr, :]], axis=0)
            s_ref[t % 2, 0:PREFIX + kr, cols] = jnp.dot(keys, qs, preferred_element_type=F32)
        else:
            s_ref[t % 2, 0:kr, cols] = jnp.dot(k_ref[j * bk:j * bk + kr, :], qs,
                                               preferred_element_type=F32)

    def update(t, c):
        i, j = steps[t]
        par = i % 2
        cols = slice(c * ATT_STRIP, (c + 1) * ATT_STRIP)
        kr = key_rows(c) if j == i else bk
        if j == 0:
            s = s_ref[t % 2, 0:PREFIX + kr, cols]
            body = s[PREFIX:, :]
            if i == 0:
                body = jnp.where(causal(c), body, MASK_VALUE)
            s = jnp.concatenate([jnp.where(pvalid, s[0:PREFIX, :], MASK_VALUE), body], axis=0)
            vtb = jnp.concatenate([pvt_ref[0], vt_ref[0, :, 0:kr]], axis=1)
            m_new = jnp.max(s, axis=0, keepdims=True)
            p = jnp.exp2(s - m_new)
            m_ref[par, :, cols] = m_new
            l_ref[par, :, cols] = jnp.sum(p, axis=0, keepdims=True)
            acc_ref[par, :, cols] = jnp.dot(vtb, p.astype(BF16), preferred_element_type=F32)
        else:
            s = s_ref[t % 2, 0:kr, cols]
            if j == i:
                s = jnp.where(causal(c), s, MASK_VALUE)
            m = m_ref[par, :, cols]
            m_new = jnp.maximum(m, jnp.max(s, axis=0, keepdims=True))
            alpha = jnp.exp2(m - m_new)
            p = jnp.exp2(s - m_new)
            m_ref[par, :, cols] = m_new
            l_ref[par, :, cols] = alpha * l_ref[par, :, cols] + jnp.sum(p, axis=0, keepdims=True)
            acc_ref[par, :, cols] = alpha * acc_ref[par, :, cols] + jnp.dot(
                vt_ref[j, :, 0:kr], p.astype(BF16), preferred_element_type=F32)

    def epilogue(i):
        par = i % 2
        out = acc_ref[par] * (1.0 / l_ref[par])
        ot = out[:, :bq] - lam * out[:, bq:]
        ms = jnp.mean(ot * ot, axis=0, keepdims=True)
        ot = ot * (lax.rsqrt(ms + EPS) * (1.0 - LAM_INIT))
        for t in range(nsub):
            cols = slice(t * HEAD_W, (t + 1) * HEAD_W)
            o = (ot[:, cols] * gaint_ref[...]).T
            rows = slice(i * bq + t * HEAD_W, i * bq + (t + 1) * HEAD_W)
            z = z_ref[rows, :].astype(F32)
            o_ref[rows, :] = (o * (z * _sigmoid(z))).astype(BF16)

    n_strips = 2 * bq // ATT_STRIP
    build_q(0)
    for c in range(n_strips):
        issue_scores(0, c)
    for t, (i, j) in enumerate(steps):
        has_next = t + 1 < len(steps)
        if has_next and steps[t + 1][1] == 0:
            build_q(steps[t + 1][0])
        for c in range(n_strips):
            update(t, c)
            if has_next:
                issue_scores(t + 1, c)
        if j == i:
            epilogue(i)


def _attn(proj, vt, pproj, pvt, lam, gain, *, batch, seq):
    def seg_spec(seg):
        return pl.BlockSpec((seq, HEAD_W), lambda b, h: (b, seg * N_HEADS + h))

    def pseg_spec(seg):
        return pl.BlockSpec((PREFIX, HEAD_W), lambda b, h: (0, seg * N_HEADS + h))

    return pl.pallas_call(
        _attn_kernel,
        grid=(batch, N_HEADS),
        in_specs=[
            pl.BlockSpec(memory_space=pltpu.SMEM),
            pseg_spec(SEG_AK),
            pl.BlockSpec((1, HEAD_W, PREFIX), lambda b, h: (0, h, 0)),
            seg_spec(SEG_AQ), seg_spec(SEG_AK),
            pl.BlockSpec((seq // ATT_BK, HEAD_W, ATT_BK), lambda b, h: (b, h, 0)),
            seg_spec(SEG_AZ),
            pl.BlockSpec((HEAD_W, HEAD_W), lambda b, h: (0, 0)),
        ],
        out_specs=pl.BlockSpec((seq, HEAD_W), lambda b, h: (b, h)),
        out_shape=jax.ShapeDtypeStruct((batch * seq, SEG_W), BF16),
        scratch_shapes=[pltpu.VMEM((2, HEAD_W, 2 * ATT_BQ), F32),
                        pltpu.VMEM((2, HEAD_W, 2 * ATT_BQ), BF16),
                        pltpu.VMEM((2, ATT_BK + PREFIX, 2 * ATT_BQ), F32),
                        pltpu.VMEM((2, 1, 2 * ATT_BQ), F32),
                        pltpu.VMEM((2, 1, 2 * ATT_BQ), F32)],
        compiler_params=pltpu.CompilerParams(
            dimension_semantics=("parallel", "parallel"), vmem_limit_bytes=VMEM_LIMIT),
        name="diff_attn",
    )(lam, pproj, pvt, proj, proj, vt, proj, gain)


def _out_kernel(x_ref, oa_ref, ob_ref, ga_ref, gb_ref, wa_ref, wb_ref, wo_ref, fg_ref, out_ref):
    ya = jnp.dot(oa_ref[...], wa_ref[...], preferred_element_type=F32)
    yb = jnp.dot(ob_ref[...], wb_ref[...], preferred_element_type=F32)
    m = _sigmoid(ga_ref[...].astype(F32)) * ya + _sigmoid(gb_ref[...].astype(F32)) * yb
    h = x_ref[...] + jnp.dot(m.astype(BF16), wo_ref[...], preferred_element_type=F32)
    ms = jnp.mean(h * h, axis=-1, keepdims=True)
    out_ref[...] = h * lax.rsqrt(ms + EPS) * fg_ref[...]


def _out_stage(x2d, oa, ob, proj, wa, wb, wo, fg, *, tm):
    n_rows = x2d.shape[0]
    tile = lambda col: pl.BlockSpec((tm, SEG_W), lambda i: (i, col))
    wspec = pl.BlockSpec((SEG_W, D_MODEL), lambda i: (0, 0))
    return pl.pallas_call(
        _out_kernel,
        grid=(n_rows // tm,),
        in_specs=[tile(0), tile(0), tile(0), tile(SEG_GA), tile(SEG_GB), wspec, wspec, wspec,
                  pl.BlockSpec((1, D_MODEL), lambda i: (0, 0))],
        out_specs=tile(0),
        out_shape=jax.ShapeDtypeStruct((n_rows, D_MODEL), F32),
        compiler_params=pltpu.CompilerParams(
            dimension_semantics=("parallel",), vmem_limit_bytes=VMEM_LIMIT),
        name="out_stage",
    )(x2d, oa, ob, proj, proj, wa, wb, wo, fg)


def _rope_tables(pos):
    lane = jnp.arange(HEAD_W) % DA_D
    inv = ROPE_THETA ** (-(lane % ROPE_HALF).astype(F32) * 2.0 / ROPE_DIM)
    ang = pos.astype(F32)[:, None] * inv[None, :]
    cos, sin = jnp.cos(ang), jnp.sin(ang)
    lo = (lane < ROPE_HALF)[None, :]
    hi = jnp.logical_and(lane >= ROPE_HALF, lane < ROPE_DIM)[None, :]
    cos_t = jnp.where(jnp.logical_or(lo, hi), cos, 1.0)
    sa_t = jnp.where(hi, sin, 0.0)
    sb_t = jnp.where(lo, -sin, 0.0)
    return cos_t, sa_t, sb_t


def kernel(x, meta_tokens, norm_g, w_in, hg_lb_logits, hg_norm_g, da_lambda, da_norm_g,
           w_branch_a, w_branch_b, w_out, final_g):
    batch, seq, d = x.shape
    assert d == D_MODEL and seq % ATT_BQ == 0 and ATT_BQ == ATT_BK
    assert norm_g.shape[0] == 1, "single-layer block"

    x2d = x.reshape(batch * seq, d)
    prefix = jnp.concatenate([jnp.zeros((N_INERT, d), x.dtype), meta_tokens.astype(x.dtype)], axis=0)

    lb = jnp.cumsum(jax.nn.softmax(hg_lb_logits.astype(F32), axis=0), axis=0)[0][None, :]
    lp = da_lambda[0].astype(F32)
    lam = (jnp.exp(jnp.sum(lp[0] * lp[1])) - jnp.exp(jnp.sum(lp[2] * lp[3])) + LAM_INIT).reshape(1, 1)

    w_bf = w_in[0].astype(BF16)
    ng = norm_g[0][None, :].astype(F32)
    tabs_main = _rope_tables(jnp.arange(seq) + N_META)
    tabs_pre = _rope_tables(jnp.maximum(jnp.arange(PREFIX) - N_INERT, 0))

    proj, glog, gmin, vt = _inproj(x2d, ng, w_bf, lb, *tabs_main, tm=ATT_BK, rows_per_seq=seq,
                                   n_inert=0)
    pproj, pglog, pgmin, pvt = _inproj(prefix, ng, w_bf, lb, *tabs_pre, tm=PREFIX,
                                       rows_per_seq=PREFIX, n_inert=N_INERT)

    oa = _hgrn(proj, glog, gmin, pproj, pglog, pgmin, hg_norm_g[0][None, :].astype(F32),
               batch=batch, seq=seq, tile=1024)
    da_gain_rows = jnp.broadcast_to(da_norm_g[0].astype(F32)[:, None], (HEAD_W, HEAD_W))
    ob = _attn(proj, vt, pproj, pvt, lam, da_gain_rows, batch=batch, seq=seq)

    out = _out_stage(x2d, oa, ob, proj, w_branch_a[0].astype(BF16), w_branch_b[0].astype(BF16),
                     w_out[0].astype(BF16), final_g[None, :].astype(F32), tm=1024)
    return out.reshape(batch, seq, d)
```

```python
import functools

import jax
import jax.numpy as jnp
import numpy as np
from jax import lax
from jax.experimental import pallas as pl
from jax.experimental.pallas import tpu as pltpu

F32 = jnp.float32
BF16 = jnp.bfloat16

D_MODEL = 1024
N_HEADS = 8
HEAD_W = 128
DA_D = 64
N_META = 16
PREFIX = 128
N_INERT = PREFIX - N_META
ROPE_DIM = 16
ROPE_HALF = ROPE_DIM // 2
ROPE_THETA = 500000.0
EPS = 1e-6
LAM_INIT = 0.2
MASK_VALUE = -1e30
Q_SCALE = DA_D ** -0.5 * float(np.log2(np.e))

SEG_W = 1024
N_SEG = 9
SEG_HQ, SEG_HF, SEG_HI, SEG_HZ, SEG_AQ, SEG_AK, SEG_AZ, SEG_GA, SEG_GB = range(N_SEG)
REF_SEG_AV = 6

CHUNK = 128
SUB = 16
N_SUB = CHUNK // SUB
HGRN_SAFE_LOG2_DECAY = 120.0 / SUB
ATT_BQ = 512
ATT_BK = 512
ATT_STRIP = 256

VMEM_LIMIT = 56 * 1024 * 1024
INPROJ_VMEM_LIMIT = 58 * 1024 * 1024


def _sigmoid(x):
    return 1.0 / (1.0 + jnp.exp(-x))


def _inproj_kernel(n_inert, x_ref, ng_ref, w_ref, lb_ref, cos_ref, sa_ref, sb_ref,
                   out_ref, glog_ref, gmin_ref, vt_ref, u_ref):
    x = x_ref[...]
    ms = jnp.mean(x * x, axis=-1, keepdims=True)
    u_ref[...] = (x * lax.rsqrt(ms + EPS) * ng_ref[...]).astype(BF16)

    for seg in range(N_SEG):
        ref_seg = seg + (seg >= REF_SEG_AV)
        acc = jnp.dot(u_ref[...], w_ref[:, ref_seg * SEG_W:(ref_seg + 1) * SEG_W],
                      preferred_element_type=F32)
        cols = slice(seg * SEG_W, (seg + 1) * SEG_W)
        if seg == SEG_HF:
            lb = lb_ref[...]
            f = lb + (1.0 - lb) * _sigmoid(acc)
            if n_inert:
                ridx = lax.broadcasted_iota(jnp.int32, f.shape, 0)
                f = jnp.where(ridx >= n_inert, f, 1.0)
            g = jnp.log2(f)
            glog_ref[...] = g
            parts = [g[r:r + 8, :] for r in range(0, g.shape[0], 8)]
            while len(parts) > 1:
                parts = [jnp.minimum(parts[i], parts[i + 1]) for i in range(0, len(parts), 2)]
            gmin_ref[0] = parts[0]
            out_ref[:, cols] = (1.0 - f).astype(BF16)
        elif seg in (SEG_AQ, SEG_AK):
            scale = Q_SCALE if seg == SEG_AQ else 1.0
            cos = cos_ref[...] * scale
            sa = sa_ref[...] * scale
            sb = sb_ref[...] * scale
            for c in range(SEG_W // HEAD_W):
                xs = acc[:, c * HEAD_W:(c + 1) * HEAD_W]
                y = (xs * cos + pltpu.roll(xs, ROPE_HALF, 1) * sa
                     + pltpu.roll(xs, HEAD_W - ROPE_HALF, 1) * sb)
                out_ref[:, seg * SEG_W + c * HEAD_W:seg * SEG_W + (c + 1) * HEAD_W] = y.astype(BF16)
        else:
            out_ref[:, cols] = acc.astype(BF16)

    acc = jnp.dot(u_ref[...], w_ref[:, REF_SEG_AV * SEG_W:(REF_SEG_AV + 1) * SEG_W],
                  preferred_element_type=F32)
    vt_ref[0] = acc.T.astype(BF16)


def _inproj(x2d, norm_g, w_bf, lb, cos_t, sa_t, sb_t, *, tm, rows_per_seq, n_inert):
    n_rows = x2d.shape[0]
    tiles_per_seq = rows_per_seq // tm
    row_spec = pl.BlockSpec((1, D_MODEL), lambda i: (0, 0))
    tab_spec = pl.BlockSpec((tm, HEAD_W), lambda i: (i % tiles_per_seq, 0))
    return pl.pallas_call(
        functools.partial(_inproj_kernel, n_inert),
        grid=(n_rows // tm,),
        in_specs=[pl.BlockSpec((tm, D_MODEL), lambda i: (i, 0)), row_spec,
                  pl.BlockSpec(w_bf.shape, lambda i: (0, 0), pipeline_mode=pl.Buffered(1)),
                  row_spec, tab_spec, tab_spec, tab_spec],
        out_specs=[
            pl.BlockSpec((tm, N_SEG * SEG_W), lambda i: (i, 0)),
            pl.BlockSpec((tm, SEG_W), lambda i: (i, 0)),
            pl.BlockSpec((1, 8, SEG_W), lambda i: (i, 0, 0)),
            pl.BlockSpec((1, D_MODEL, tm), lambda i: (i, 0, 0)),
        ],
        out_shape=[
            jax.ShapeDtypeStruct((n_rows, N_SEG * SEG_W), BF16),
            jax.ShapeDtypeStruct((n_rows, SEG_W), F32),
            jax.ShapeDtypeStruct((n_rows // tm, 8, SEG_W), F32),
            jax.ShapeDtypeStruct((n_rows // tm, D_MODEL, tm), BF16),
        ],
        scratch_shapes=[pltpu.VMEM((tm, D_MODEL), BF16)],
        compiler_params=pltpu.CompilerParams(
            dimension_semantics=("parallel",), vmem_limit_bytes=INPROJ_VMEM_LIMIT),
        name="inproj",
    )(x2d, norm_g, w_bf, lb, cos_t, sa_t, sb_t)


def _bcast_rows(rows, n):
    return jnp.concatenate([jnp.broadcast_to(r, (n, r.shape[1])) for r in rows], axis=0)


def _head(x, h):
    return x[:, h * HEAD_W:(h + 1) * HEAD_W]


_NT = (((1,), (1,)), ((), ()))


def _hgrn_decay(q, k, g, tril):
    width = q.shape[1]
    g_hi = g.astype(BF16)
    g_lo = (g - g_hi.astype(F32)).astype(BF16)
    bs = jnp.dot(tril.astype(BF16), jnp.concatenate([g_hi, g_lo], axis=1),
                 preferred_element_type=F32)
    b = bs[:, :width] + bs[:, width:]

    ends = [b[SUB * s + SUB - 1:SUB * s + SUB, :] for s in range(N_SUB)]
    starts = [jnp.zeros_like(ends[0])] + ends[:-1]
    b_last = ends[-1]
    qd = q * jnp.exp2(b - _bcast_rows(starts, SUB))
    khat = k * jnp.exp2(_bcast_rows(ends, SUB) - b)
    qd_bf = qd.astype(BF16)
    khat_bf = khat.astype(BF16)
    q_s = qd_bf * _bcast_rows([jnp.exp2(s) for s in starts], SUB).astype(BF16)
    k_s = khat_bf * _bcast_rows([jnp.exp2(b_last - e) for e in ends], SUB).astype(BF16)
    return dict(starts=starts, ends=ends, qd=qd_bf, khat=khat_bf, q_s=q_s, k_s=k_s,
                decay=jnp.exp2(b_last), b=b, q=q, k=k)


def _hgrn_exact_diag(d, i, b_scr, k_scr):
    rows = slice(SUB * i, SUB * i + SUB)
    b_i = d["b"][rows, :]
    q_i = d["q"][rows, :]
    b_scr[...] = b_i
    k_scr[...] = d["k"][rows, :]
    lane = lax.broadcasted_iota(jnp.int32, (SUB, HEAD_W), 1)

    def body(s, accs):
        bs = b_scr[pl.ds(s, 1), :]
        ks = k_scr[pl.ds(s, 1), :]
        z = q_i * (ks * jnp.exp2(jnp.minimum(b_i - bs, 0.0)))
        return tuple(jnp.where(lane == s, jnp.sum(_head(z, h), axis=1, keepdims=True), accs[h])
                     for h in range(N_HEADS))

    return lax.fori_loop(0, SUB, body, tuple(jnp.zeros((SUB, HEAD_W), F32) for _ in range(N_HEADS)))


def _hgrn_scores(d, tril, exact=None):
    starts, ends, khat_bf, qd_bf = d["starts"], d["ends"], d["khat"], d["qd"]
    width = khat_bf.shape[1]
    a_rows = [[] for _ in range(N_HEADS)]
    for i in range(N_SUB):
        parts = []
        for jj in range(i + 1):
            kj = khat_bf[SUB * jj:SUB * jj + SUB, :]
            if jj == i:
                if exact is not None:
                    parts.append(jnp.zeros_like(kj))
                    continue
                w = jnp.exp2(starts[i] - ends[i])
            elif jj == i - 1:
                w = None
            else:
                w = jnp.exp2(starts[i] - ends[jj])
            parts.append(kj if w is None else kj * jnp.broadcast_to(w, kj.shape).astype(BF16))
        if i + 1 < N_SUB:
            parts.append(jnp.zeros((CHUNK - SUB * (i + 1), width), BF16))
        km = jnp.concatenate(parts, axis=0)
        qi = qd_bf[SUB * i:SUB * i + SUB, :]
        diag = _hgrn_exact_diag(d, i, *exact) if exact is not None else None
        for h in range(N_HEADS):
            row = lax.dot_general(_head(qi, h), _head(km, h), _NT, preferred_element_type=F32)
            if diag is not None:
                row = row + (pltpu.roll(diag[h], SUB * i, 1) if i else diag[h])
            a_rows[h].append(row)
    return [jnp.where(tril, jnp.concatenate(a_rows[h], axis=0), 0.0).astype(BF16)
            for h in range(N_HEADS)]


def _hgrn_apply(d, a, v, sts):
    outs, st_new = [], []
    for h in range(N_HEADS):
        o_inter = lax.dot_general(_head(d["q_s"], h), sts[h].astype(BF16), _NT,
                                  preferred_element_type=F32)
        st_new.append(sts[h] * _head(d["decay"], h)
                      + lax.dot_general(_head(v, h), _head(d["k_s"], h), (((0,), (0,)), ((), ())),
                                        preferred_element_type=F32))
        outs.append(jnp.dot(a[h], _head(v, h), preferred_element_type=F32) + o_inter)
    return outs, st_new


def _hgrn_kernel(pq_ref, pk_ref, pv_ref, pg_ref, pgmin_ref, q_ref, k_ref, v_ref, z_ref, g_ref, gmin_ref,
                 gain_ref, o_ref, st_ref, b_scr, k_scr):
    n_chunks = q_ref.shape[0] // CHUNK
    tril = (lax.broadcasted_iota(jnp.int32, (CHUNK, CHUNK), 0)
            >= lax.broadcasted_iota(jnp.int32, (CHUNK, CHUNK), 1))
    gain = gain_ref[...]

    def prefix(exact):
        d = _hgrn_decay(pq_ref[...].astype(F32), pk_ref[...].astype(F32), pg_ref[...], tril)
        _, sts = _hgrn_apply(d, _hgrn_scores(d, tril, exact), pv_ref[...],
                             [jnp.zeros((HEAD_W, HEAD_W), F32)] * N_HEADS)
        for h in range(N_HEADS):
            st_ref[h] = sts[h]

    first_tile = pl.program_id(1) == 0
    strong_prefix_decay = jnp.min(pgmin_ref[...]) < -HGRN_SAFE_LOG2_DECAY

    @pl.when(jnp.logical_and(first_tile, jnp.logical_not(strong_prefix_decay)))
    def _():
        prefix(None)

    @pl.when(jnp.logical_and(first_tile, strong_prefix_decay))
    def _():
        prefix((b_scr, k_scr))

    def decay(c):
        rows = slice(c * CHUNK, (c + 1) * CHUNK)
        return _hgrn_decay(q_ref[rows, :].astype(F32), k_ref[rows, :].astype(F32), g_ref[rows, :], tril)

    def tile(exact):
        sts = [st_ref[h] for h in range(N_HEADS)]
        d = decay(0)
        a = _hgrn_scores(d, tril, exact)
        for c in range(n_chunks):
            rows = slice(c * CHUNK, (c + 1) * CHUNK)
            d_next = decay(c + 1) if c + 1 < n_chunks else None
            outs, sts = _hgrn_apply(d, a, v_ref[rows, :], sts)
            a_next = _hgrn_scores(d_next, tril, exact) if d_next is not None else None
            for h in range(N_HEADS):
                o = outs[h]
                cols = slice(h * HEAD_W, (h + 1) * HEAD_W)
                ms = jnp.mean(o * o, axis=-1, keepdims=True)
                on = o * lax.rsqrt(ms + EPS) * gain
                z = z_ref[rows, cols].astype(F32)
                o_ref[rows, cols] = (on * (z * _sigmoid(z))).astype(BF16)
            d, a = d_next, a_next
        for h in range(N_HEADS):
            st_ref[h] = sts[h]

    strong_decay = jnp.min(gmin_ref[...]) < -HGRN_SAFE_LOG2_DECAY

    @pl.when(jnp.logical_not(strong_decay))
    def _():
        tile(None)

    @pl.when(strong_decay)
    def _():
        tile((b_scr, k_scr))


def _hgrn(proj, glog, gmin, pproj, pglog, pgmin, gain, *, batch, seq, tile):
    tiles_per_seq = seq // tile
    gmin_per_tile = gmin.shape[0] * tile // (batch * seq)

    def seg_spec(seg):
        return pl.BlockSpec((tile, SEG_W), lambda b, t: (b * tiles_per_seq + t, seg))

    def pseg_spec(seg):
        return pl.BlockSpec((PREFIX, SEG_W), lambda b, t: (0, seg))

    return pl.pallas_call(
        _hgrn_kernel,
        grid=(batch, tiles_per_seq),
        in_specs=[
            pseg_spec(SEG_HQ), pseg_spec(SEG_HF), pseg_spec(SEG_HI), pseg_spec(0),
            pl.BlockSpec(pgmin.shape, lambda b, t: (0, 0, 0)),
            seg_spec(SEG_HQ), seg_spec(SEG_HF), seg_spec(SEG_HI), seg_spec(SEG_HZ), seg_spec(0),
            pl.BlockSpec((gmin_per_tile, 8, SEG_W), lambda b, t: (b * tiles_per_seq + t, 0, 0)),
            pl.BlockSpec((1, HEAD_W), lambda b, t: (0, 0)),
        ],
        out_specs=seg_spec(0),
        out_shape=jax.ShapeDtypeStruct((batch * seq, SEG_W), BF16),
        scratch_shapes=[pltpu.VMEM((N_HEADS, HEAD_W, HEAD_W), F32),
                        pltpu.VMEM((SUB, SEG_W), F32), pltpu.VMEM((SUB, SEG_W), F32)],
        compiler_params=pltpu.CompilerParams(
            dimension_semantics=("parallel", "arbitrary"), vmem_limit_bytes=VMEM_LIMIT),
        name="hgrn2",
    )(pproj, pproj, pproj, pglog, pgmin, proj, proj, proj, proj, glog, gmin, gain)


def _attn_kernel(lam_ref, pk_ref, pvt_ref, q_ref, k_ref, vt_ref, z_ref, gaint_ref, o_ref,
                 acc_ref, qqt_ref, s_ref, m_ref, l_ref):
    bq, bk = ATT_BQ, ATT_BK
    nsub = bq // HEAD_W
    n_q = q_ref.shape[0] // bq
    lam = lam_ref[0, 0]
    steps = [(i, j) for i in range(n_q) for j in range(i + 1)]

    drow = lax.broadcasted_iota(jnp.int32, (HEAD_W, HEAD_W), 0)
    first = drow < DA_D
    pvalid = lax.broadcasted_iota(jnp.int32, (PREFIX, ATT_STRIP), 0) >= N_INERT

    def causal(c):
        kr = key_rows(c)
        krow = lax.broadcasted_iota(jnp.int32, (kr, ATT_STRIP), 0)
        query = lax.broadcasted_iota(jnp.int32, (kr, ATT_STRIP), 1) + (c * ATT_STRIP) % bq
        return krow <= query

    def key_rows(c):
        return min(bk, (c * ATT_STRIP) % bq + ATT_STRIP)

    def build_q(i):
        qts = [q_ref[i * bq + t * HEAD_W:i * bq + (t + 1) * HEAD_W, :].astype(F32).T
               for t in range(nsub)]
        qqt_ref[i % 2] = jnp.concatenate(
            [jnp.where(first, qt, 0.0) for qt in qts] + [jnp.where(first, 0.0, qt) for qt in qts],
            axis=1).astype(BF16)

    def issue_scores(t, c):
        i, j = steps[t]
        cols = slice(c * ATT_STRIP, (c + 1) * ATT_STRIP)
        qs = qqt_ref[i % 2, :, cols]
        kr = key_rows(c) if j == i else bk
        if j == 0:
            keys = jnp.concatenate([pk_ref[...], k_ref[0:kr, :]], axis=0)
            s_ref[t % 2, 0:PREFIX + kr, cols] = jnp.dot(keys, qs, preferred_element_type=F32)
        else:
            s_ref[t % 2, 0:kr, cols] = jnp.dot(k_ref[j * bk:j * bk + kr, :], qs,
                                               preferred_element_type=F32)

    def update(t, c):
        i, j = steps[t]
        par = i % 2
        cols = slice(c * ATT_STRIP, (c + 1) * ATT_STRIP)
        kr = key_rows(c) if j == i else bk
        if j == 0:
            s = s_ref[t % 2, 0:PREFIX + kr, cols]
            body = s[PREFIX:, :]
            if i == 0:
                body = jnp.where(causal(c), body, MASK_VALUE)
            s = jnp.concatenate([jnp.where(pvalid, s[0:PREFIX, :], MASK_VALUE), body], axis=0)
            vtb = jnp.concatenate([pvt_ref[0], vt_ref[0, :, 0:kr]], axis=1)
            m_new = jnp.max(s, axis=0, keepdims=True)
            p = jnp.exp2(s - m_new)
            m_ref[par, :, cols] = m_new
            l_ref[par, :, cols] = jnp.sum(p, axis=0, keepdims=True)
            acc_ref[par, :, cols] = jnp.dot(vtb, p.astype(BF16), preferred_element_type=F32)
        else:
            s = s_ref[t % 2, 0:kr, cols]
            if j == i:
                s = jnp.where(causal(c), s, MASK_VALUE)
            m = m_ref[par, :, cols]
            m_new = jnp.maximum(m, jnp.max(s, axis=0, keepdims=True))
            alpha = jnp.exp2(m - m_new)
            p = jnp.exp2(s - m_new)
            m_ref[par, :, cols] = m_new
            l_ref[par, :, cols] = alpha * l_ref[par, :, cols] + jnp.sum(p, axis=0, keepdims=True)
            acc_ref[par, :, cols] = alpha * acc_ref[par, :, cols] + jnp.dot(
                vt_ref[j, :, 0:kr], p.astype(BF16), preferred_element_type=F32)

    def epilogue(i):
        par = i % 2
        out = acc_ref[par] * (1.0 / l_ref[par])
        ot = out[:, :bq] - lam * out[:, bq:]
        ms = jnp.mean(ot * ot, axis=0, keepdims=True)
        ot = ot * (lax.rsqrt(ms + EPS) * (1.0 - LAM_INIT))
        for t in range(nsub):
            cols = slice(t * HEAD_W, (t + 1) * HEAD_W)
            o = (ot[:, cols] * gaint_ref[...]).T
            rows = slice(i * bq + t * HEAD_W, i * bq + (t + 1) * HEAD_W)
            z = z_ref[rows, :].astype(F32)
            o_ref[rows, :] = (o * (z * _sigmoid(z))).astype(BF16)

    n_strips = 2 * bq // ATT_STRIP
    build_q(0)
    for c in range(n_strips):
        issue_scores(0, c)
    for t, (i, j) in enumerate(steps):
        has_next = t + 1 < len(steps)
        if has_next and steps[t + 1][1] == 0:
            build_q(steps[t + 1][0])
        for c in range(n_strips):
            update(t, c)
            if has_next and c >= 1:
                issue_scores(t + 1, c - 1)
        if has_next:
            issue_scores(t + 1, n_strips - 1)
        if j == i:
            epilogue(i)


def _attn(proj, vt, pproj, pvt, lam, gain, *, batch, seq):
    def seg_spec(seg):
        return pl.BlockSpec((seq, HEAD_W), lambda b, h: (b, seg * N_HEADS + h))

    def pseg_spec(seg):
        return pl.BlockSpec((PREFIX, HEAD_W), lambda b, h: (0, seg * N_HEADS + h))

    return pl.pallas_call(
        _attn_kernel,
        grid=(batch, N_HEADS),
        in_specs=[
            pl.BlockSpec(memory_space=pltpu.SMEM),
            pseg_spec(SEG_AK),
            pl.BlockSpec((1, HEAD_W, PREFIX), lambda b, h: (0, h, 0)),
            seg_spec(SEG_AQ), seg_spec(SEG_AK),
            pl.BlockSpec((seq // ATT_BK, HEAD_W, ATT_BK), lambda b, h: (b, h, 0)),
            seg_spec(SEG_AZ),
            pl.BlockSpec((HEAD_W, HEAD_W), lambda b, h: (0, 0)),
        ],
        out_specs=pl.BlockSpec((seq, HEAD_W), lambda b, h: (b, h)),
        out_shape=jax.ShapeDtypeStruct((batch * seq, SEG_W), BF16),
        scratch_shapes=[pltpu.VMEM((2, HEAD_W, 2 * ATT_BQ), F32),
                        pltpu.VMEM((2, HEAD_W, 2 * ATT_BQ), BF16),
                        pltpu.VMEM((2, ATT_BK + PREFIX, 2 * ATT_BQ), F32),
                        pltpu.VMEM((2, 1, 2 * ATT_BQ), F32),
                        pltpu.VMEM((2, 1, 2 * ATT_BQ), F32)],
        compiler_params=pltpu.CompilerParams(
            dimension_semantics=("parallel", "parallel"), vmem_limit_bytes=VMEM_LIMIT),
        name="diff_attn",
    )(lam, pproj, pvt, proj, proj, vt, proj, gain)


def _out_kernel(x_ref, oa_ref, ob_ref, ga_ref, gb_ref, wa_ref, wb_ref, wo_ref, fg_ref, out_ref):
    ya = jnp.dot(oa_ref[...], wa_ref[...], preferred_element_type=F32)
    yb = jnp.dot(ob_ref[...], wb_ref[...], preferred_element_type=F32)
    m = _sigmoid(ga_ref[...].astype(F32)) * ya + _sigmoid(gb_ref[...].astype(F32)) * yb
    h = x_ref[...] + jnp.dot(m.astype(BF16), wo_ref[...], preferred_element_type=F32)
    ms = jnp.mean(h * h, axis=-1, keepdims=True)
    out_ref[...] = h * lax.rsqrt(ms + EPS) * fg_ref[...]


def _out_stage(x2d, oa, ob, proj, wa, wb, wo, fg, *, tm):
    n_rows = x2d.shape[0]
    tile = lambda col: pl.BlockSpec((tm, SEG_W), lambda i: (i, col))
    wspec = pl.BlockSpec((SEG_W, D_MODEL), lambda i: (0, 0))
    return pl.pallas_call(
        _out_kernel,
        grid=(n_rows // tm,),
        in_specs=[tile(0), tile(0), tile(0), tile(SEG_GA), tile(SEG_GB), wspec, wspec, wspec,
                  pl.BlockSpec((1, D_MODEL), lambda i: (0, 0))],
        out_specs=tile(0),
        out_shape=jax.ShapeDtypeStruct((n_rows, D_MODEL), F32),
        compiler_params=pltpu.CompilerParams(
            dimension_semantics=("parallel",), vmem_limit_bytes=VMEM_LIMIT),
        name="out_stage",
    )(x2d, oa, ob, proj, proj, wa, wb, wo, fg)


def _rope_tables(pos):
    lane = jnp.arange(HEAD_W) % DA_D
    inv = ROPE_THETA ** (-(lane % ROPE_HALF).astype(F32) * 2.0 / ROPE_DIM)
    ang = pos.astype(F32)[:, None] * inv[None, :]
    cos, sin = jnp.cos(ang), jnp.sin(ang)
    lo = (lane < ROPE_HALF)[None, :]
    hi = jnp.logical_and(lane >= ROPE_HALF, lane < ROPE_DIM)[None, :]
    cos_t = jnp.where(jnp.logical_or(lo, hi), cos, 1.0)
    sa_t = jnp.where(hi, sin, 0.0)
    sb_t = jnp.where(lo, -sin, 0.0)
    return cos_t, sa_t, sb_t


def kernel(x, meta_tokens, norm_g, w_in, hg_lb_logits, hg_norm_g, da_lambda, da_norm_g,
           w_branch_a, w_branch_b, w_out, final_g):
    batch, seq, d = x.shape
    assert d == D_MODEL and seq % ATT_BQ == 0 and ATT_BQ == ATT_BK
    assert norm_g.shape[0] == 1, "single-layer block"

    x2d = x.reshape(batch * seq, d)
    prefix = jnp.concatenate([jnp.zeros((N_INERT, d), x.dtype), meta_tokens.astype(x.dtype)], axis=0)

    lb = jnp.cumsum(jax.nn.softmax(hg_lb_logits.astype(F32), axis=0), axis=0)[0][None, :]
    lp = da_lambda[0].astype(F32)
    lam = (jnp.exp(jnp.sum(lp[0] * lp[1])) - jnp.exp(jnp.sum(lp[2] * lp[3])) + LAM_INIT).reshape(1, 1)

    w_bf = w_in[0].astype(BF16)
    ng = norm_g[0][None, :].astype(F32)
    tabs_main = _rope_tables(jnp.arange(seq) + N_META)
    tabs_pre = _rope_tables(jnp.maximum(jnp.arange(PREFIX) - N_INERT, 0))

    proj, glog, gmin, vt = _inproj(x2d, ng, w_bf, lb, *tabs_main, tm=ATT_BK, rows_per_seq=seq,
                                   n_inert=0)
    pproj, pglog, pgmin, pvt = _inproj(prefix, ng, w_bf, lb, *tabs_pre, tm=PREFIX,
                                       rows_per_seq=PREFIX, n_inert=N_INERT)

    oa = _hgrn(proj, glog, gmin, pproj, pglog, pgmin, hg_norm_g[0][None, :].astype(F32),
               batch=batch, seq=seq, tile=1024)
    da_gain_rows = jnp.broadcast_to(da_norm_g[0].astype(F32)[:, None], (HEAD_W, HEAD_W))
    ob = _attn(proj, vt, pproj, pvt, lam, da_gain_rows, batch=batch, seq=seq)

    out = _out_stage(x2d, oa, ob, proj, w_branch_a[0].astype(BF16), w_branch_b[0].astype(BF16),
                     w_out[0].astype(BF16), final_g[None, :].astype(F32), tm=1024)
    return out.reshape(batch, seq, d)
```
